```python
import jax
import jax.numpy as jnp
from jax import lax
import numpy as np

D_MODEL = 1024
BATCH = 16
SEQ = 2048
DEPTH = 2

HEAD_DIM = 64
ROPE_THETA = 10000.0
RMS_EPS = 1e-6
N_BRANCH = 3

A_HEADS = 6
A_WIDTH = A_HEADS * HEAD_DIM
A_BLOCK = 256
A_TOPK = 3
A_QCHUNK = 32

B_HEADS = 6
B_WIDTH = B_HEADS * HEAD_DIM
B_KV_DIM = HEAD_DIM
B_TOPK = 256
B_IDX_HEADS = 4
B_IDX_DIM = HEAD_DIM
B_QCHUNK = 128

C_GROUPS = ((128, 1), (512, 4), (2048, 16))
C_SLOTS = 4
C_HEADS = C_SLOTS * len(C_GROUPS)
C_WIDTH = C_SLOTS * HEAD_DIM
C_BLOCK = 128

IN_SPLITS = (A_WIDTH, A_WIDTH, A_WIDTH, A_WIDTH,
             B_WIDTH, B_KV_DIM, B_KV_DIM, B_WIDTH, B_IDX_HEADS * B_IDX_DIM, B_IDX_DIM, B_IDX_HEADS,
             C_HEADS * HEAD_DIM, C_HEADS * HEAD_DIM, C_HEADS * HEAD_DIM, C_WIDTH,
             N_BRANCH * D_MODEL)
IN_WIDTH = sum(IN_SPLITS)
IN_OFFSETS = tuple(int(o) for o in np.cumsum(IN_SPLITS)[:-1])

kernel_name = 'hybrid_moba_dsa_dilated_gated'


def rmsnorm(x, g):
    xf = x.astype(jnp.float32)
    y = xf * lax.rsqrt(jnp.mean(xf * xf, axis=-1, keepdims=True) + RMS_EPS)
    return (y * g.astype(jnp.float32)).astype(x.dtype)


def rope_tables(seq_len):
    inv_freq = 1.0 / (ROPE_THETA ** (jnp.arange(0, HEAD_DIM, 2, dtype=jnp.float32) / HEAD_DIM))
    ang = jnp.arange(seq_len, dtype=jnp.float32)[:, None] * inv_freq[None, :]
    return jnp.cos(ang), jnp.sin(ang)


def apply_rope(x, cos, sin):
    half = HEAD_DIM // 2
    c = cos[None, :, None, :].astype(x.dtype)
    s = sin[None, :, None, :].astype(x.dtype)
    x1, x2 = x[..., :half], x[..., half:]
    return jnp.concatenate([x1 * c - x2 * s, x1 * s + x2 * c], axis=-1)


def moba_attention(q, k, v):
    bsz, seq, nh, hd = q.shape
    scale = hd ** -0.5
    n_blk = -(-seq // A_BLOCK)
    pad = n_blk * A_BLOCK - seq
    qh = q.transpose(0, 2, 1, 3)
    kb = jnp.pad(k.transpose(0, 2, 1, 3), ((0, 0), (0, 0), (0, pad), (0, 0))).reshape(bsz, nh, n_blk, A_BLOCK, hd)
    vb = jnp.pad(v.transpose(0, 2, 1, 3), ((0, 0), (0, 0), (0, pad), (0, 0))).reshape(bsz, nh, n_blk, A_BLOCK, hd)
    k_mean = jnp.mean(kb.astype(jnp.float32), axis=3)
    topk = max(1, min(A_TOPK, n_blk - 1))
    b_ix = jnp.arange(bsz)[:, None, None, None]
    h_ix = jnp.arange(nh)[None, :, None, None]
    blk_ids = jnp.arange(n_blk)
    n_sel = topk * A_BLOCK

    def chunk(ci):
        t0 = ci * A_QCHUNK
        own = t0 // A_BLOCK
        t_pos = t0 + jnp.arange(A_QCHUNK)
        qc = lax.dynamic_slice_in_dim(qh, t0, A_QCHUNK, axis=2)
        k_own = lax.dynamic_index_in_dim(kb, own, axis=2, keepdims=False)
        v_own = lax.dynamic_index_in_dim(vb, own, axis=2, keepdims=False)
        kpos_own = own * A_BLOCK + jnp.arange(A_BLOCK)
        s_own = jnp.einsum('bhqd,bhkd->bhqk', qc, k_own, preferred_element_type=jnp.float32) * scale
        s_own = jnp.where(kpos_own[None, :] <= t_pos[:, None], s_own, -jnp.inf)
        gate = jnp.einsum('bhqd,bhnd->bhqn', qc.astype(jnp.float32), k_mean)
        gate = jnp.where(blk_ids < own, gate, -jnp.inf)
        g_val, g_idx = lax.top_k(gate, topk)
        sel_ok = jnp.isfinite(g_val)
        k_sel = kb[b_ix, h_ix, g_idx]
        v_sel = vb[b_ix, h_ix, g_idx]
        s_sel = jnp.einsum('bhqd,bhqnkd->bhqnk', qc, k_sel, preferred_element_type=jnp.float32) * scale
        s_sel = jnp.where(sel_ok[..., None], s_sel, -jnp.inf)
        scores = jnp.concatenate([s_sel.reshape(bsz, nh, A_QCHUNK, n_sel), s_own], axis=-1)
        p = jax.nn.softmax(scores, axis=-1).astype(v.dtype)
        p_sel = p[..., :n_sel].reshape(bsz, nh, A_QCHUNK, topk, A_BLOCK)
        out = jnp.einsum('bhqnk,bhqnkd->bhqd', p_sel, v_sel)
        return out + jnp.einsum('bhqk,bhkd->bhqd', p[..., n_sel:], v_own)

    outs = lax.map(chunk, jnp.arange(seq // A_QCHUNK))
    return outs.transpose(1, 0, 3, 2, 4).reshape(bsz, seq, nh, hd)


def dsa_attention(q, k, v, iq, ik, iw):
    bsz, seq, nh, hd = q.shape
    scale = hd ** -0.5
    n_keep = min(B_TOPK, seq // 4)
    b_ix = jnp.arange(bsz)[:, None, None]
    key_pos = jnp.arange(seq)
    iw = iw.astype(jnp.float32) * (B_IDX_HEADS * B_IDX_DIM) ** -0.5

    def chunk(ci):
        t0 = ci * B_QCHUNK
        t_pos = t0 + jnp.arange(B_QCHUNK)
        iq_c = lax.dynamic_slice_in_dim(iq, t0, B_QCHUNK, axis=1)
        iw_c = lax.dynamic_slice_in_dim(iw, t0, B_QCHUNK, axis=1)
        logit = jnp.einsum('bqhd,bsd->bqhs', iq_c, ik, preferred_element_type=jnp.float32)
        idx_score = jnp.einsum('bqhs,bqh->bqs', jax.nn.relu(logit), iw_c)
        idx_score = jnp.where(key_pos[None, :] <= t_pos[:, None], idx_score, -jnp.inf)
        _, sel = lax.top_k(idx_score, n_keep)
        sel_ok = sel <= t_pos[None, :, None]
        k_g = k[b_ix, sel]
        v_g = v[b_ix, sel]
        qc = lax.dynamic_slice_in_dim(q, t0, B_QCHUNK, axis=1)
        s = jnp.einsum('bqhd,bqkd->bqhk', qc, k_g, preferred_element_type=jnp.float32) * scale
        s = jnp.where(sel_ok[:, :, None, :], s, -jnp.inf)
        p = jax.nn.softmax(s, axis=-1).astype(v.dtype)
        return jnp.einsum('bqhk,bqkd->bqhd', p, v_g)

    outs = lax.map(chunk, jnp.arange(seq // B_QCHUNK))
    return outs.transpose(1, 0, 2, 3, 4).reshape(bsz, seq, nh, hd)


def dilated_group(q, k, v, steps, dil):
    bsz, seq, nh, hd = q.shape
    n_sub = seq // dil
    n_blk = -(-n_sub // C_BLOCK)
    pad = n_blk * C_BLOCK - n_sub

    def to_sub(t):
        t = t.reshape(bsz, n_sub, dil, nh, hd).transpose(0, 2, 3, 1, 4)
        return jnp.pad(t, ((0, 0), (0, 0), (0, 0), (0, pad), (0, 0)))

    def band(t):
        tb = jnp.pad(to_sub(t), ((0, 0), (0, 0), (0, 0), (C_BLOCK, 0), (0, 0)))
        tb = tb.reshape(bsz, dil, nh, n_blk + 1, C_BLOCK, hd)
        return jnp.concatenate([tb[:, :, :, :-1], tb[:, :, :, 1:]], axis=4)

    qs = to_sub(q).reshape(bsz, dil, nh, n_blk, C_BLOCK, hd)
    kw = band(k)
    vw = band(v)
    s = jnp.einsum('brhnqd,brhnkd->brhnqk', qs, kw, preferred_element_type=jnp.float32) * hd ** -0.5
    blk = jnp.arange(n_blk)[:, None, None]
    qi = jnp.arange(C_BLOCK)[None, :, None]
    ki = jnp.arange(2 * C_BLOCK)[None, None, :]
    dist = C_BLOCK + qi - ki
    ok = (dist >= 0) & (dist <= steps) & (blk * C_BLOCK + ki >= C_BLOCK)
    s = jnp.where(ok, s, -jnp.inf)
    lse = jax.nn.logsumexp(s, axis=-1)
    p = jnp.exp(s - lse[..., None]).astype(v.dtype)
    o = jnp.einsum('brhnqk,brhnkd->brhnqd', p, vw)
    o = o.reshape(bsz, dil, nh, n_blk * C_BLOCK, hd)[:, :, :, :n_sub]
    o = o.transpose(0, 3, 1, 2, 4).reshape(bsz, seq, nh, hd)
    lse = lse.reshape(bsz, dil, nh, n_blk * C_BLOCK)[..., :n_sub]
    lse = lse.transpose(0, 3, 1, 2).reshape(bsz, seq, nh)
    return o, lse


def dilated_mixture(q, k, v):
    outs = []
    lses = []
    for g, (window, dil) in enumerate(C_GROUPS):
        sl = slice(g * C_SLOTS, (g + 1) * C_SLOTS)
        o, l = dilated_group(q[:, :, sl], k[:, :, sl], v[:, :, sl], window // dil, dil)
        outs.append(o)
        lses.append(l)
    w = jax.nn.softmax(jnp.stack(lses, axis=0), axis=0)
    return jnp.sum(w[..., None] * jnp.stack(outs, axis=0), axis=0).astype(q.dtype)


def hybrid_layer(x, norm_g, w_in, w_br_a, w_br_b, w_br_c, w_out, cos, sin):
    bsz, seq, _ = x.shape
    h = rmsnorm(x, norm_g)
    proj = h @ w_in
    (a_q, a_k, a_v, a_g, b_q, b_k, b_v, b_g, i_q, i_k, i_w,
     c_q, c_k, c_v, c_g, m_g) = jnp.split(proj, IN_OFFSETS, axis=-1)

    def heads(t, n):
        return t.reshape(bsz, seq, n, -1)

    def rope1(t):
        return apply_rope(t[:, :, None, :], cos, sin)[:, :, 0, :]

    ya = moba_attention(apply_rope(heads(a_q, A_HEADS), cos, sin),
                        apply_rope(heads(a_k, A_HEADS), cos, sin),
                        heads(a_v, A_HEADS))
    ya = ya.reshape(bsz, seq, A_WIDTH) * jax.nn.silu(a_g)
    yb = dsa_attention(apply_rope(heads(b_q, B_HEADS), cos, sin), rope1(b_k), b_v,
                       apply_rope(heads(i_q, B_IDX_HEADS), cos, sin), rope1(i_k), i_w)
    yb = yb.reshape(bsz, seq, B_WIDTH) * jax.nn.silu(b_g)
    yc = dilated_mixture(apply_rope(heads(c_q, C_HEADS), cos, sin),
                         apply_rope(heads(c_k, C_HEADS), cos, sin),
                         heads(c_v, C_HEADS))
    yc = yc.reshape(bsz, seq, C_WIDTH) * jax.nn.silu(c_g)
    gates = jax.nn.sigmoid(m_g).reshape(bsz, seq, N_BRANCH, D_MODEL)
    merged = (gates[:, :, 0] * (ya @ w_br_a)
              + gates[:, :, 1] * (yb @ w_br_b)
              + gates[:, :, 2] * (yc @ w_br_c))
    return x + merged @ w_out


def setup_inputs(seed: int = 0) -> dict:
    key = jax.random.key(seed)
    ks = jax.random.split(key, 8)
    x = jax.random.normal(ks[0], (BATCH, SEQ, D_MODEL), jnp.float32)
    norm_g = 1.0 + 0.02 * jax.random.normal(ks[1], (DEPTH, D_MODEL), jnp.float32)
    w_in = jax.random.normal(ks[2], (DEPTH, D_MODEL, IN_WIDTH), jnp.float32) * D_MODEL ** -0.5
    w_br_a = jax.random.normal(ks[3], (DEPTH, A_WIDTH, D_MODEL), jnp.float32) * A_WIDTH ** -0.5
    w_br_b = jax.random.normal(ks[4], (DEPTH, B_WIDTH, D_MODEL), jnp.float32) * B_WIDTH ** -0.5
    w_br_c = jax.random.normal(ks[5], (DEPTH, C_WIDTH, D_MODEL), jnp.float32) * C_WIDTH ** -0.5
    w_out = jax.random.normal(ks[6], (DEPTH, D_MODEL, D_MODEL), jnp.float32) * D_MODEL ** -0.5
    final_norm_g = 1.0 + 0.02 * jax.random.normal(ks[7], (D_MODEL,), jnp.float32)
    return {'x': x, 'norm_g': norm_g, 'w_in': w_in, 'w_br_a': w_br_a, 'w_br_b': w_br_b,
            'w_br_c': w_br_c, 'w_out': w_out, 'final_norm_g': final_norm_g}


def reference(x, norm_g, w_in, w_br_a, w_br_b, w_br_c, w_out, final_norm_g):
    cos, sin = rope_tables(x.shape[1])
    for layer in range(DEPTH):
        x = hybrid_layer(x, norm_g[layer], w_in[layer], w_br_a[layer], w_br_b[layer],
                         w_br_c[layer], w_out[layer], cos, sin)
    return rmsnorm(x, final_norm_g)
```

```python
import functools

import numpy as np
import jax
import jax.numpy as jnp
from jax import lax
from jax.experimental import pallas as pl
from jax.experimental.pallas import tpu as pltpu

D_MODEL = 1024
SEQ = 2048
DEPTH = 2
HEAD_DIM = 64
ROPE_THETA = 10000.0
RMS_EPS = 1e-6
N_BRANCH = 3

A_HEADS = 6
A_WIDTH = A_HEADS * HEAD_DIM
A_BLOCK = 256
A_TOPK = 3

B_HEADS = 6
B_WIDTH = B_HEADS * HEAD_DIM
B_TOPK = 256
B_IDX_HEADS = 4
B_IDX_DIM = HEAD_DIM

C_GROUPS = ((128, 1), (512, 4), (2048, 16))
C_SLOTS = 4
C_HEADS = C_SLOTS * len(C_GROUPS)
C_WIDTH = C_SLOTS * HEAD_DIM
C_BLOCK = 128

IN_SPLITS = (A_WIDTH, A_WIDTH, A_WIDTH, A_WIDTH,
             B_WIDTH, HEAD_DIM, HEAD_DIM, B_WIDTH, B_IDX_HEADS * B_IDX_DIM, B_IDX_DIM, B_IDX_HEADS,
             C_HEADS * HEAD_DIM, C_HEADS * HEAD_DIM, C_HEADS * HEAD_DIM, C_WIDTH,
             N_BRANCH * D_MODEL)
_OFF = tuple(int(o) for o in np.cumsum((0,) + IN_SPLITS))

LANES = 128
VMEM_LIMIT = 56 * 1024 * 1024

_MAIN_GROUPS = (("a_q", A_WIDTH), ("a_k", A_WIDTH), ("b_q", B_WIDTH), ("b_kk", LANES),
                ("i_q", B_IDX_HEADS * B_IDX_DIM), ("i_kk", LANES),
                ("c_q", C_HEADS * HEAD_DIM), ("c_k", C_HEADS * HEAD_DIM), ("c_v", C_HEADS * HEAD_DIM))
_MAIN_OFF = {}
_c = 0
for _n, _w in _MAIN_GROUPS:
    _MAIN_OFF[_n] = _c
    _c += _w
MAIN_WIDTH = _c
ROPE_WIDTH = _MAIN_OFF["c_v"]
VT_ROWS = A_WIDTH + LANES
IWT_ROWS = 8
GATE_WIDTH = A_WIDTH + B_WIDTH + C_WIDTH + N_BRANCH * D_MODEL

TM_PROJ = 512
TM_MERGE = 512
TQ_DSA = 128
INT_MIN = -2 ** 31
NEG_INF = float("-inf")


def _dot(a, b):
    return jnp.dot(a, b, preferred_element_type=jnp.float32)


def _dot_nt(a, b):
    return lax.dot_general(a, b, (((1,), (1,)), ((), ())), preferred_element_type=jnp.float32)


def _rms(x, g):
    return x * lax.rsqrt(jnp.mean(x * x, axis=-1, keepdims=True) + RMS_EPS) * g


def _head_mask(hh):
    lane = lax.broadcasted_iota(jnp.int32, (1, LANES), 1)
    return (lane >= hh * HEAD_DIM) & (lane < (hh + 1) * HEAD_DIM)


def _proj_kernel(x_ref, g_ref, w_ref, wt_ref, cos_ref, sin_ref, main_ref, vt_ref, iwt_ref):
    h = _rms(x_ref[...], g_ref[...]).astype(jnp.bfloat16)
    cos = cos_ref[...]
    sin = sin_ref[...]
    lane = lax.broadcasted_iota(jnp.int32, (1, LANES), 1)
    first_half = (lane % HEAD_DIM) < (HEAD_DIM // 2)
    chunk = 4 * LANES
    for c0 in range(0, MAIN_WIDTH, chunk):
        w = min(chunk, MAIN_WIDTH - c0)
        res = _dot(h, w_ref[:, c0:c0 + w])
        for j in range(0, w, LANES):
            blk = res[:, j:j + LANES]
            if c0 + j < ROPE_WIDTH:
                partner = jnp.where(first_half, pltpu.roll(blk, LANES - HEAD_DIM // 2, 1),
                                    pltpu.roll(blk, HEAD_DIM // 2, 1))
                blk = blk * cos + partner * sin
            main_ref[:, c0 + j:c0 + j + LANES] = blk.astype(main_ref.dtype)
    rt = _dot_nt(wt_ref[...], h)
    vt_ref[...] = rt[:VT_ROWS].astype(vt_ref.dtype)
    iwt_ref[...] = rt[VT_ROWS:]


def _proj(x2, g, w_main, w_t, cos_t, sin_t):
    t = x2.shape[0]
    n_rope_tiles = SEQ // TM_PROJ
    return pl.pallas_call(
        _proj_kernel,
        grid=(t // TM_PROJ,),
        in_specs=[
            pl.BlockSpec((TM_PROJ, D_MODEL), lambda i: (i, 0)),
            pl.BlockSpec((1, D_MODEL), lambda i: (0, 0)),
            pl.BlockSpec((D_MODEL, MAIN_WIDTH), lambda i: (0, 0)),
            pl.BlockSpec((VT_ROWS + IWT_ROWS, D_MODEL), lambda i: (0, 0)),
            pl.BlockSpec((TM_PROJ, LANES), lambda i: (i % n_rope_tiles, 0)),
            pl.BlockSpec((TM_PROJ, LANES), lambda i: (i % n_rope_tiles, 0)),
        ],
        out_specs=[
            pl.BlockSpec((TM_PROJ, MAIN_WIDTH), lambda i: (i, 0)),
            pl.BlockSpec((VT_ROWS, TM_PROJ), lambda i: (0, i)),
            pl.BlockSpec((IWT_ROWS, TM_PROJ), lambda i: (0, i)),
        ],
        out_shape=[
            jax.ShapeDtypeStruct((t, MAIN_WIDTH), jnp.bfloat16),
            jax.ShapeDtypeStruct((VT_ROWS, t), jnp.bfloat16),
            jax.ShapeDtypeStruct((IWT_ROWS, t), jnp.float32),
        ],
        compiler_params=pltpu.CompilerParams(dimension_semantics=("parallel",),
                                             vmem_limit_bytes=VMEM_LIMIT),
        name="proj",
    )(x2, g, w_main, w_t, cos_t, sin_t)


N_ABLK = SEQ // A_BLOCK


def _moba_kernel(q_ref, k_ref, vt_ref, o_ref, kmean_ref, vblk_ref, sel_ref):
    qt = pl.program_id(2)

    @pl.when(qt == 0)
    def _():
        for n in range(N_ABLK):
            kb = k_ref[n * A_BLOCK:(n + 1) * A_BLOCK, :].astype(jnp.float32)
            kmean_ref[n:n + 1, :] = jnp.mean(kb, axis=0, keepdims=True)
            vblk_ref[n] = vt_ref[:, n * A_BLOCK:(n + 1) * A_BLOCK]

    q = q_ref[...]
    kmean = kmean_ref[...].astype(jnp.bfloat16)
    blk_ids = lax.broadcasted_iota(jnp.int32, (N_ABLK, A_BLOCK), 0)
    blk_f = blk_ids.astype(jnp.float32)
    kpos = lax.broadcasted_iota(jnp.int32, (A_BLOCK, A_BLOCK), 0)
    qpos = lax.broadcasted_iota(jnp.int32, (A_BLOCK, A_BLOCK), 1)
    causal = kpos <= qpos
    own_start = pl.multiple_of(qt * A_BLOCK, A_BLOCK)
    outs = []
    for hh in range(2):
        qh = jnp.where(_head_mask(hh), q, jnp.zeros_like(q))
        gate = _dot_nt(kmean, qh)
        gate = jnp.where(blk_ids < qt, gate, NEG_INF)
        sel = jnp.zeros(gate.shape, jnp.float32)
        for _ in range(A_TOPK):
            top = jnp.max(gate, axis=0, keepdims=True)
            is_top = (gate == top) & (top > NEG_INF)
            idx = jnp.min(jnp.where(is_top, blk_f, float(N_ABLK)), axis=0, keepdims=True)
            pick = blk_f == idx
            sel = jnp.where(pick, 1.0, sel)
            gate = jnp.where(pick, NEG_INF, gate)
        sel_ref[hh] = sel
        s = _dot_nt(k_ref[pl.ds(own_start, A_BLOCK), :], qh)
        s = jnp.where(causal, s, NEG_INF)
        m_i = jnp.max(s, axis=0, keepdims=True)
        p = jnp.exp(s - m_i)
        l_i = jnp.sum(p, axis=0, keepdims=True)
        acc = _dot(vblk_ref[qt], p.astype(jnp.bfloat16))

        def past_block(n, carry, qh=qh, hh=hh):
            m_i, l_i, acc = carry
            start = pl.multiple_of(n * A_BLOCK, A_BLOCK)
            s = _dot_nt(k_ref[pl.ds(start, A_BLOCK), :], qh)
            s = jnp.where(sel_ref[hh, pl.ds(n, 1), :] > 0.0, s, NEG_INF)
            m_new = jnp.maximum(m_i, jnp.max(s, axis=0, keepdims=True))
            p = jnp.exp(s - m_new)
            corr = jnp.exp(m_i - m_new)
            l_new = l_i * corr + jnp.sum(p, axis=0, keepdims=True)
            acc = acc * corr + _dot(vblk_ref[n], p.astype(jnp.bfloat16))
            return m_new, l_new, acc

        m_i, l_i, acc = lax.fori_loop(0, qt, past_block, (m_i, l_i, acc))
        o = acc / l_i
        outs.append(o[hh * HEAD_DIM:(hh + 1) * HEAD_DIM])
    o_ref[...] = jnp.concatenate(outs, axis=0).T.astype(o_ref.dtype)


def _moba(main, vt, bsz):
    t = main.shape[0]
    n_pairs = A_HEADS // 2
    q_col = _MAIN_OFF["a_q"] // LANES
    k_col = _MAIN_OFF["a_k"] // LANES
    return pl.pallas_call(
        _moba_kernel,
        grid=(bsz, n_pairs, N_ABLK),
        in_specs=[
            pl.BlockSpec((A_BLOCK, LANES), lambda b, p, i: (b * N_ABLK + i, q_col + p)),
            pl.BlockSpec((SEQ, LANES), lambda b, p, i: (b, k_col + p)),
            pl.BlockSpec((LANES, SEQ), lambda b, p, i: (p, b)),
        ],
        out_specs=pl.BlockSpec((A_BLOCK, LANES), lambda b, p, i: (b * N_ABLK + i, p)),
        out_shape=jax.ShapeDtypeStruct((t, A_WIDTH), jnp.bfloat16),
        scratch_shapes=[
            pltpu.VMEM((N_ABLK, LANES), jnp.float32),
            pltpu.VMEM((N_ABLK, LANES, A_BLOCK), jnp.bfloat16),
            pltpu.VMEM((2, N_ABLK, A_BLOCK), jnp.float32),
        ],
        compiler_params=pltpu.CompilerParams(
            dimension_semantics=("parallel", "parallel", "arbitrary"), vmem_limit_bytes=VMEM_LIMIT),
        name="moba",
    )(main, main, vt)


TIE_CHUNK = 256


def _dsa_kernel(ikk_ref, kk_ref, vvt_ref, iq_ref, iwt_ref, q_ref, o_ref, u_ref, bias_ref):
    qt = pl.program_id(1)
    n_keep = float(B_TOPK)
    key_pos = lax.broadcasted_iota(jnp.int32, (SEQ, TQ_DSA), 0)
    q_pos = qt * TQ_DSA + lax.broadcasted_iota(jnp.int32, (SEQ, TQ_DSA), 1)
    causal = key_pos <= q_pos

    iq = iq_ref[...]
    iw = iwt_ref[...]
    ikk = ikk_ref[...]
    score = jnp.zeros((SEQ, TQ_DSA), jnp.float32)
    for h in range(B_IDX_HEADS):
        blk = iq[:, (h // 2) * LANES:(h // 2 + 1) * LANES]
        qh = jnp.where(_head_mask(h % 2), blk, jnp.zeros_like(blk))
        score = score + jnp.maximum(_dot_nt(ikk, qh), 0.0) * iw[h:h + 1, :]

    score = jnp.where(score == 0.0, 0.0, score)
    bits = pltpu.bitcast(score, jnp.int32)
    u = jnp.where(bits < 0, bits ^ jnp.int32(0x7FFFFFFF), bits)
    u_ref[...] = jnp.where(causal, u, jnp.int32(INT_MIN))

    def bisect(i, tau):
        trial = tau ^ jnp.left_shift(jnp.int32(1), 31 - i)
        cnt = jnp.sum(jnp.where(u_ref[...] >= trial, 1.0, 0.0), axis=0, keepdims=True)
        return jnp.where(cnt >= n_keep, trial, tau)

    tau = lax.fori_loop(0, 32, bisect, jnp.full((1, TQ_DSA), INT_MIN, jnp.int32))
    bounded = tau > jnp.int32(INT_MIN)

    ge = u_ref[...] >= jnp.maximum(tau, jnp.int32(INT_MIN + 1))
    n_ge = jnp.sum(jnp.where(ge, 1.0, 0.0), axis=0, keepdims=True)
    bias_ref[...] = jnp.where(ge, 0.0, NEG_INF)
    excess = jnp.where(bounded, n_ge - n_keep, 0.0)

    @pl.when(jnp.max(excess) > 0.0)
    def _():
        n_gt = jnp.sum(jnp.where(u_ref[...] > tau, 1.0, 0.0), axis=0, keepdims=True)
        need = n_keep - n_gt
        r_i = lax.broadcasted_iota(jnp.int32, (TIE_CHUNK, TIE_CHUNK), 0)
        c_i = lax.broadcasted_iota(jnp.int32, (TIE_CHUNK, TIE_CHUNK), 1)
        below = jnp.where(c_i < r_i, 1.0, 0.0).astype(jnp.bfloat16)
        seen = jnp.zeros((1, TQ_DSA), jnp.float32)
        for c in range(SEQ // TIE_CHUNK):
            uc = u_ref[c * TIE_CHUNK:(c + 1) * TIE_CHUNK, :]
            eq = jnp.where((uc == tau) & bounded, 1.0, 0.0)
            rank = _dot(below, eq.astype(jnp.bfloat16)) + seen
            seen = seen + jnp.sum(eq, axis=0, keepdims=True)
            keep = (uc > tau) | ((eq > 0.0) & (rank < need))
            bias_ref[c * TIE_CHUNK:(c + 1) * TIE_CHUNK, :] = jnp.where(keep, 0.0, NEG_INF)

    q = q_ref[...]
    kk = kk_ref[...]
    vvt = vvt_ref[...]
    bias = bias_ref[...]
    outs = []
    for h in range(B_HEADS):
        blk = q[:, (h // 2) * LANES:(h // 2 + 1) * LANES]
        qh = jnp.where(_head_mask(h % 2), blk, jnp.zeros_like(blk))
        s = _dot_nt(kk, qh) + bias
        m = jnp.max(s, axis=0, keepdims=True)
        p = jnp.exp(s - m)
        l = jnp.sum(p, axis=0, keepdims=True)
        o = _dot(vvt, p.astype(jnp.bfloat16)) / l
        outs.append(o[(h % 2) * HEAD_DIM:(h % 2 + 1) * HEAD_DIM])
    o_ref[...] = jnp.concatenate(outs, axis=0).T.astype(o_ref.dtype)


def _dsa(main, vt, iwt, bsz):
    t = main.shape[0]
    n_qt = SEQ // TQ_DSA
    return pl.pallas_call(
        _dsa_kernel,
        grid=(bsz, n_qt),
        in_specs=[
            pl.BlockSpec((SEQ, LANES), lambda b, i: (b, _MAIN_OFF["i_kk"] // LANES)),
            pl.BlockSpec((SEQ, LANES), lambda b, i: (b, _MAIN_OFF["b_kk"] // LANES)),
            pl.BlockSpec((LANES, SEQ), lambda b, i: (A_WIDTH // LANES, b)),
            pl.BlockSpec((TQ_DSA, 2 * LANES), lambda b, i: (b * n_qt + i, _MAIN_OFF["i_q"] // (2 * LANES))),
            pl.BlockSpec((IWT_ROWS, TQ_DSA), lambda b, i: (0, b * n_qt + i)),
            pl.BlockSpec((TQ_DSA, B_WIDTH), lambda b, i: (b * n_qt + i, _MAIN_OFF["b_q"] // B_WIDTH)),
        ],
        out_specs=pl.BlockSpec((TQ_DSA, B_WIDTH), lambda b, i: (b * n_qt + i, 0)),
        out_shape=jax.ShapeDtypeStruct((t, B_WIDTH), jnp.bfloat16),
        scratch_shapes=[
            pltpu.VMEM((SEQ, TQ_DSA), jnp.int32),
            pltpu.VMEM((SEQ, TQ_DSA), jnp.float32),
        ],
        compiler_params=pltpu.CompilerParams(
            dimension_semantics=("parallel", "arbitrary"), vmem_limit_bytes=VMEM_LIMIT),
        name="dsa",
    )(main, main, vt, main, iwt, main)


N_CBLK = SEQ // C_BLOCK
N_GROUPS = len(C_GROUPS)


def _dilated_kernel(*refs):
    q_in = refs[0:N_GROUPS]
    k_in = refs[N_GROUPS:2 * N_GROUPS]
    v_in = refs[2 * N_GROUPS:3 * N_GROUPS]
    o_ref = refs[3 * N_GROUPS]
    scratch = refs[3 * N_GROUPS + 1:]
    tmp_ref = scratch[0]
    dense = scratch[1:1 + 3 * (N_GROUPS - 1)]
    od_ref, ld_ref = scratch[-2 - 2 * N_GROUPS], scratch[-1 - 2 * N_GROUPS]
    o_tok = scratch[-2 * N_GROUPS:-N_GROUPS]
    l_tok = scratch[-N_GROUPS:]

    lane = lax.broadcasted_iota(jnp.int32, (1, LANES), 1)
    low = lane < HEAD_DIM
    qi = lax.broadcasted_iota(jnp.int32, (C_BLOCK, C_BLOCK), 0)
    ki = lax.broadcasted_iota(jnp.int32, (C_BLOCK, C_BLOCK), 1)
    own_ok = ki <= qi

    for g, (window, dil) in enumerate(C_GROUPS):
        assert window // dil == C_BLOCK
        n_sub = SEQ // dil
        n_blk = n_sub // C_BLOCK
        if dil == 1:
            qd, kd, vd = q_in[g], k_in[g], v_in[g]
        else:
            qd, kd, vd = dense[3 * (g - 1):3 * g]
            for src, dst in ((q_in[g], qd), (k_in[g], kd), (v_in[g], vd)):
                tmp_ref[...] = src[...].astype(jnp.float32)
                for r in range(dil):
                    dst[r * n_sub:(r + 1) * n_sub, :] = tmp_ref[pl.ds(r, n_sub, stride=dil), :].astype(dst.dtype)
        o_dst = o_tok[g] if dil == 1 else od_ref
        l_dst = l_tok[g] if dil == 1 else ld_ref

        def block(it, carry, qd=qd, kd=kd, vd=vd, o_dst=o_dst, l_dst=l_dst, n_blk=n_blk):
            start = pl.multiple_of(it * C_BLOCK, C_BLOCK)
            qs = qd[pl.ds(start, C_BLOCK), :]
            k_own = kd[pl.ds(start, C_BLOCK), :]
            v_own = vd[pl.ds(start, C_BLOCK), :]
            if n_blk > 1:
                pstart = pl.multiple_of(jnp.maximum(it - 1, 0) * C_BLOCK, C_BLOCK)
                k_prev = kd[pl.ds(pstart, C_BLOCK), :]
                v_prev = vd[pl.ds(pstart, C_BLOCK), :]
                prev_ok = ki >= qi + jnp.where((it % n_blk) > 0, 0, C_BLOCK)
            o_h, l_h = [], []
            for hh in range(2):
                qh = jnp.where(_head_mask(hh), qs, jnp.zeros_like(qs))
                s = jnp.where(own_ok, _dot_nt(qh, k_own), NEG_INF)
                vv = v_own
                if n_blk > 1:
                    s_prev = jnp.where(prev_ok, _dot_nt(qh, k_prev), NEG_INF)
                    s = jnp.concatenate([s_prev, s], axis=1)
                    vv = jnp.concatenate([v_prev, v_own], axis=0)
                m = jnp.max(s, axis=1, keepdims=True)
                p = jnp.exp(s - m)
                l = jnp.sum(p, axis=1, keepdims=True)
                o_h.append(_dot(p.astype(jnp.bfloat16), vv) / l)
                l_h.append(m + jnp.log(l))
            o_dst[pl.ds(start, C_BLOCK), :] = jnp.where(low, o_h[0], o_h[1])
            l_dst[pl.ds(start, C_BLOCK), :] = jnp.where(low, l_h[0], l_h[1])
            return carry

        lax.fori_loop(0, N_CBLK, block, 0)
        if dil > 1:
            for r in range(dil):
                o_tok[g][pl.ds(r, n_sub, stride=dil), :] = od_ref[r * n_sub:(r + 1) * n_sub, :]
                l_tok[g][pl.ds(r, n_sub, stride=dil), :] = ld_ref[r * n_sub:(r + 1) * n_sub, :]

    lses = [l_tok[g][...] for g in range(N_GROUPS)]
    top = functools.reduce(jnp.maximum, lses)
    es = [jnp.exp(x - top) for x in lses]
    num = sum(e * o_tok[g][...] for g, e in enumerate(es))
    o_ref[...] = (num / sum(es)).astype(o_ref.dtype)


def _dilated(main, bsz):
    t = main.shape[0]
    n_pairs = C_SLOTS // 2

    def col_spec(name, g):
        base = _MAIN_OFF[name] // LANES + g * n_pairs
        return pl.BlockSpec((SEQ, LANES), lambda b, p: (b, base + p))

    in_specs = ([col_spec("c_q", g) for g in range(N_GROUPS)]
                + [col_spec("c_k", g) for g in range(N_GROUPS)]
                + [col_spec("c_v", g) for g in range(N_GROUPS)])
    scratch = ([pltpu.VMEM((SEQ, LANES), jnp.float32)]
               + [pltpu.VMEM((SEQ, LANES), jnp.bfloat16)] * (3 * (N_GROUPS - 1))
               + [pltpu.VMEM((SEQ, LANES), jnp.float32)] * (2 + 2 * N_GROUPS))
    return pl.pallas_call(
        _dilated_kernel,
        grid=(bsz, n_pairs),
        in_specs=in_specs,
        out_specs=pl.BlockSpec((SEQ, LANES), lambda b, p: (b, p)),
        out_shape=jax.ShapeDtypeStruct((t, C_WIDTH), jnp.bfloat16),
        scratch_shapes=scratch,
        compiler_params=pltpu.CompilerParams(
            dimension_semantics=("parallel", "parallel"), vmem_limit_bytes=VMEM_LIMIT),
        name="dilated",
    )(*([main] * (3 * N_GROUPS)))


def _sigmoid(x):
    return 1.0 / (1.0 + jnp.exp(-x))


def _merge_kernel(x_ref, g_ref, ya_ref, yb_ref, yc_ref, wg_ref, wa_ref, wb_ref, wc_ref, wo_ref, fg_ref,
                  o_ref, *, final):
    x = x_ref[...]
    h = _rms(x, g_ref[...]).astype(jnp.bfloat16)
    merged = jnp.zeros(x.shape, jnp.float32)
    c0 = 0
    m0 = A_WIDTH + B_WIDTH + C_WIDTH
    for i, (y_ref, w_ref) in enumerate(((ya_ref, wa_ref), (yb_ref, wb_ref), (yc_ref, wc_ref))):
        width = y_ref.shape[1]
        gate = _dot(h, wg_ref[:, c0:c0 + width])
        z = (y_ref[...].astype(jnp.float32) * (gate * _sigmoid(gate))).astype(jnp.bfloat16)
        mix = _sigmoid(_dot(h, wg_ref[:, m0 + i * D_MODEL:m0 + (i + 1) * D_MODEL]))
        merged = merged + mix * _dot(z, w_ref[...])
        c0 += width
    out = x + _dot(merged.astype(jnp.bfloat16), wo_ref[...])
    if final:
        out = _rms(out, fg_ref[...])
    o_ref[...] = out


def _merge(x2, g, ya, yb, yc, w_gate, w_a, w_b, w_c, w_o, fg, final):
    t = x2.shape[0]
    row = lambda w: pl.BlockSpec((TM_MERGE, w), lambda i: (i, 0))
    full = lambda a: pl.BlockSpec(a.shape, lambda i: (0, 0))
    return pl.pallas_call(
        functools.partial(_merge_kernel, final=final),
        grid=(t // TM_MERGE,),
        in_specs=[row(D_MODEL), full(g), row(A_WIDTH), row(B_WIDTH), row(C_WIDTH),
                  full(w_gate), full(w_a), full(w_b), full(w_c), full(w_o), full(fg)],
        out_specs=row(D_MODEL),
        out_shape=jax.ShapeDtypeStruct((t, D_MODEL), jnp.float32),
        compiler_params=pltpu.CompilerParams(dimension_semantics=("parallel",),
                                             vmem_limit_bytes=VMEM_LIMIT),
        name="merge_final" if final else "merge",
    )(x2, g, ya, yb, yc, w_gate, w_a, w_b, w_c, w_o, fg)


def _rope_tables():
    inv_freq = 1.0 / (ROPE_THETA ** (jnp.arange(0, HEAD_DIM, 2, dtype=jnp.float32) / HEAD_DIM))
    ang = jnp.arange(SEQ, dtype=jnp.float32)[:, None] * inv_freq[None, :]
    cos, sin = jnp.cos(ang), jnp.sin(ang)
    reps = LANES // HEAD_DIM
    cos_t = jnp.tile(jnp.concatenate([cos, cos], axis=1), (1, reps))
    sin_t = jnp.tile(jnp.concatenate([-sin, sin], axis=1), (1, reps))
    return cos_t, sin_t


def _split_w_in(w):
    names = ("a_q", "a_k", "a_v", "a_g", "b_q", "b_k", "b_v", "b_g", "i_q", "i_k", "i_w",
             "c_q", "c_k", "c_v", "c_g", "m_g")
    return {n: w[:, _OFF[i]:_OFF[i + 1]] for i, n in enumerate(names)}


def _layer_weights(w_in):
    p = _split_w_in(w_in)
    scale = HEAD_DIM ** -0.5
    main = jnp.concatenate([
        p["a_q"] * scale, p["a_k"], p["b_q"] * scale, p["b_k"], p["b_k"],
        p["i_q"], p["i_k"], p["i_k"], p["c_q"] * scale, p["c_k"], p["c_v"]], axis=1)
    iw_scale = (B_IDX_HEADS * B_IDX_DIM) ** -0.5
    w_t = jnp.concatenate([
        p["a_v"].T, p["b_v"].T, p["b_v"].T, p["i_w"].T * iw_scale,
        jnp.zeros((IWT_ROWS - B_IDX_HEADS, D_MODEL), w_in.dtype)], axis=0)
    gates = jnp.concatenate([p["a_g"], p["b_g"], p["c_g"], p["m_g"]], axis=1)
    bf = jnp.bfloat16
    return main.astype(bf), w_t.astype(bf), gates.astype(bf)


def kernel(x, norm_g, w_in, w_br_a, w_br_b, w_br_c, w_out, final_norm_g):
    bsz, seq, d_model = x.shape
    assert seq == SEQ and d_model == D_MODEL
    assert MAIN_WIDTH % LANES == 0 and GATE_WIDTH == sum(IN_SPLITS[i] for i in (3, 7, 14, 15))
    cos_t, sin_t = _rope_tables()
    bf = jnp.bfloat16
    x2 = x.reshape(bsz * seq, d_model)
    fg = final_norm_g.reshape(1, d_model)
    for layer in range(DEPTH):
        w_main, w_t, w_gate = _layer_weights(w_in[layer])
        g = norm_g[layer].reshape(1, d_model)
        main, vt, iwt = _proj(x2, g, w_main, w_t, cos_t, sin_t)
        ya = _moba(main, vt, bsz)
        yb = _dsa(main, vt, iwt, bsz)
        yc = _dilated(main, bsz)
        x2 = _merge(x2, g, ya, yb, yc, w_gate, w_br_a[layer].astype(bf), w_br_b[layer].astype(bf),
                    w_br_c[layer].astype(bf), w_out[layer].astype(bf), fg, final=(layer == DEPTH - 1))
    return x2.reshape(bsz, seq, d_model)
```

```python
import functools

import numpy as np
import jax
import jax.numpy as jnp
from jax import lax
from jax.experimental import pallas as pl
from jax.experimental.pallas import tpu as pltpu

D_MODEL = 1024
SEQ = 2048
DEPTH = 2
HEAD_DIM = 64
ROPE_THETA = 10000.0
RMS_EPS = 1e-6
N_BRANCH = 3

A_HEADS = 6
A_WIDTH = A_HEADS * HEAD_DIM
A_BLOCK = 256
A_TOPK = 3

B_HEADS = 6
B_WIDTH = B_HEADS * HEAD_DIM
B_TOPK = 256
B_IDX_HEADS = 4
B_IDX_DIM = HEAD_DIM

C_GROUPS = ((128, 1), (512, 4), (2048, 16))
C_SLOTS = 4
C_HEADS = C_SLOTS * len(C_GROUPS)
C_WIDTH = C_SLOTS * HEAD_DIM
C_BLOCK = 128

IN_SPLITS = (A_WIDTH, A_WIDTH, A_WIDTH, A_WIDTH,
             B_WIDTH, HEAD_DIM, HEAD_DIM, B_WIDTH, B_IDX_HEADS * B_IDX_DIM, B_IDX_DIM, B_IDX_HEADS,
             C_HEADS * HEAD_DIM, C_HEADS * HEAD_DIM, C_HEADS * HEAD_DIM, C_WIDTH,
             N_BRANCH * D_MODEL)
_OFF = tuple(int(o) for o in np.cumsum((0,) + IN_SPLITS))

LANES = 128
VMEM_LIMIT = 56 * 1024 * 1024

_MAIN_GROUPS = (("a_q", A_WIDTH), ("a_k", A_WIDTH), ("b_q", B_WIDTH), ("b_kk", LANES),
                ("i_q", B_IDX_HEADS * B_IDX_DIM), ("i_kk", LANES),
                ("c_q", C_HEADS * HEAD_DIM), ("c_k", C_HEADS * HEAD_DIM), ("c_v", C_HEADS * HEAD_DIM))
_MAIN_OFF = {}
_c = 0
for _n, _w in _MAIN_GROUPS:
    _MAIN_OFF[_n] = _c
    _c += _w
MAIN_WIDTH = _c
ROPE_WIDTH = _MAIN_OFF["c_v"]
VT_ROWS = A_WIDTH + LANES
IWT_ROWS = 8
GATE_WIDTH = A_WIDTH + B_WIDTH + C_WIDTH + N_BRANCH * D_MODEL

TM_PROJ = 512
TM_MERGE = 512
TQ_DSA = 128
INT_MIN = -2 ** 31
NEG_INF = float("-inf")


def _dot(a, b):
    return jnp.dot(a, b, preferred_element_type=jnp.float32)


def _dot_nt(a, b):
    return lax.dot_general(a, b, (((1,), (1,)), ((), ())), preferred_element_type=jnp.float32)


def _colsum(x):
    n = x.shape[0]
    part = x.reshape(n // LANES, LANES, x.shape[1]).sum(axis=0)
    return part.sum(axis=0, keepdims=True)


def _colmax(x):
    n = x.shape[0]
    part = x.reshape(n // LANES, LANES, x.shape[1]).max(axis=0)
    return part.max(axis=0, keepdims=True)


def _rms(x, g):
    return x * lax.rsqrt(jnp.mean(x * x, axis=-1, keepdims=True) + RMS_EPS) * g


def _head_mask(hh):
    lane = lax.broadcasted_iota(jnp.int32, (1, LANES), 1)
    return (lane >= hh * HEAD_DIM) & (lane < (hh + 1) * HEAD_DIM)


def _proj_kernel(x_ref, g_ref, w_ref, wt_ref, cos_ref, sin_ref, main_ref, vt_ref, iwt_ref):
    h = _rms(x_ref[...], g_ref[...]).astype(jnp.bfloat16)
    cos = cos_ref[...]
    sin = sin_ref[...]
    lane = lax.broadcasted_iota(jnp.int32, (1, LANES), 1)
    first_half = (lane % HEAD_DIM) < (HEAD_DIM // 2)
    chunk = 4 * LANES
    for c0 in range(0, MAIN_WIDTH, chunk):
        w = min(chunk, MAIN_WIDTH - c0)
        res = _dot(h, w_ref[:, c0:c0 + w])
        for j in range(0, w, LANES):
            blk = res[:, j:j + LANES]
            if c0 + j < ROPE_WIDTH:
                partner = jnp.where(first_half, pltpu.roll(blk, LANES - HEAD_DIM // 2, 1),
                                    pltpu.roll(blk, HEAD_DIM // 2, 1))
                blk = blk * cos + partner * sin
            main_ref[:, c0 + j:c0 + j + LANES] = blk.astype(main_ref.dtype)
    rt = _dot_nt(wt_ref[...], h)
    vt_ref[...] = rt[:VT_ROWS].astype(vt_ref.dtype)
    iwt_ref[...] = rt[VT_ROWS:]


def _proj(x2, g, w_main, w_t, cos_t, sin_t):
    t = x2.shape[0]
    n_rope_tiles = SEQ // TM_PROJ
    return pl.pallas_call(
        _proj_kernel,
        grid=(t // TM_PROJ,),
        in_specs=[
            pl.BlockSpec((TM_PROJ, D_MODEL), lambda i: (i, 0)),
            pl.BlockSpec((1, D_MODEL), lambda i: (0, 0)),
            pl.BlockSpec((D_MODEL, MAIN_WIDTH), lambda i: (0, 0)),
            pl.BlockSpec((VT_ROWS + IWT_ROWS, D_MODEL), lambda i: (0, 0)),
            pl.BlockSpec((TM_PROJ, LANES), lambda i: (i % n_rope_tiles, 0)),
            pl.BlockSpec((TM_PROJ, LANES), lambda i: (i % n_rope_tiles, 0)),
        ],
        out_specs=[
            pl.BlockSpec((TM_PROJ, MAIN_WIDTH), lambda i: (i, 0)),
            pl.BlockSpec((VT_ROWS, TM_PROJ), lambda i: (0, i)),
            pl.BlockSpec((IWT_ROWS, TM_PROJ), lambda i: (0, i)),
        ],
        out_shape=[
            jax.ShapeDtypeStruct((t, MAIN_WIDTH), jnp.bfloat16),
            jax.ShapeDtypeStruct((VT_ROWS, t), jnp.bfloat16),
            jax.ShapeDtypeStruct((IWT_ROWS, t), jnp.float32),
        ],
        compiler_params=pltpu.CompilerParams(dimension_semantics=("parallel",),
                                             vmem_limit_bytes=VMEM_LIMIT),
        name="proj",
    )(x2, g, w_main, w_t, cos_t, sin_t)


N_ABLK = SEQ // A_BLOCK


def _moba_kernel(q_ref, k_ref, vt_ref, o_ref, kmean_ref):
    qt = pl.program_id(2)

    @pl.when(qt == 0)
    def _():
        for n in range(N_ABLK):
            kb = k_ref[n * A_BLOCK:(n + 1) * A_BLOCK, :].astype(jnp.float32)
            kmean_ref[n:n + 1, :] = jnp.mean(kb, axis=0, keepdims=True)

    for own in range(N_ABLK):
        @pl.when(qt == own)
        def _(own=own):
            _moba_body(own, q_ref, k_ref, vt_ref, o_ref, kmean_ref)


def _moba_body(own, q_ref, k_ref, vt_ref, o_ref, kmean_ref):
    nk = (own + 1) * A_BLOCK
    q = q_ref[...]
    kmean = kmean_ref[...].astype(jnp.bfloat16)
    blk_f = lax.broadcasted_iota(jnp.int32, (N_ABLK, A_BLOCK), 0).astype(jnp.float32)
    kpos = lax.broadcasted_iota(jnp.int32, (A_BLOCK, A_BLOCK), 0)
    qpos = lax.broadcasted_iota(jnp.int32, (A_BLOCK, A_BLOCK), 1)
    causal = kpos <= qpos
    keys = k_ref[0:nk, :]
    vt = vt_ref[:, 0:nk]
    outs = []
    for hh in range(2):
        qh = jnp.where(_head_mask(hh), q, jnp.zeros_like(q))
        gate = _dot_nt(kmean, qh)
        gate = jnp.where(blk_f < float(own), gate, NEG_INF)
        sel = jnp.zeros(gate.shape, jnp.float32)
        for _ in range(min(A_TOPK, own)):
            top = jnp.max(gate, axis=0, keepdims=True)
            is_top = (gate == top) & (top > NEG_INF)
            idx = jnp.min(jnp.where(is_top, blk_f, float(N_ABLK)), axis=0, keepdims=True)
            pick = blk_f == idx
            sel = jnp.where(pick, 1.0, sel)
            gate = jnp.where(pick, NEG_INF, gate)
        s = _dot_nt(keys, qh)
        parts = [jnp.where(sel[n:n + 1, :] > 0.0, s[n * A_BLOCK:(n + 1) * A_BLOCK], NEG_INF)
                 for n in range(own)]
        parts.append(jnp.where(causal, s[own * A_BLOCK:], NEG_INF))
        s = jnp.concatenate(parts, axis=0)
        m = _colmax(s)
        p = jnp.exp(s - m)
        l = _colsum(p)
        o = _dot(vt, p.astype(jnp.bfloat16)) / l
        outs.append(o[hh * HEAD_DIM:(hh + 1) * HEAD_DIM])
    o_ref[...] = jnp.concatenate(outs, axis=0).T.astype(o_ref.dtype)


def _moba(main, vt, bsz):
    t = main.shape[0]
    n_pairs = A_HEADS // 2
    q_col = _MAIN_OFF["a_q"] // LANES
    k_col = _MAIN_OFF["a_k"] // LANES
    return pl.pallas_call(
        _moba_kernel,
        grid=(bsz, n_pairs, N_ABLK),
        in_specs=[
            pl.BlockSpec((A_BLOCK, LANES), lambda b, p, i: (b * N_ABLK + i, q_col + p)),
            pl.BlockSpec((SEQ, LANES), lambda b, p, i: (b, k_col + p)),
            pl.BlockSpec((LANES, SEQ), lambda b, p, i: (p, b)),
        ],
        out_specs=pl.BlockSpec((A_BLOCK, LANES), lambda b, p, i: (b * N_ABLK + i, p)),
        out_shape=jax.ShapeDtypeStruct((t, A_WIDTH), jnp.bfloat16),
        scratch_shapes=[pltpu.VMEM((N_ABLK, LANES), jnp.float32)],
        compiler_params=pltpu.CompilerParams(
            dimension_semantics=("parallel", "parallel", "arbitrary"), vmem_limit_bytes=VMEM_LIMIT),
        name="moba",
    )(main, main, vt)


TIE_CHUNK = 256


KV_STEP = 512


def _dsa_kernel(ikk_ref, kk_ref, vvt_ref, iq_ref, iwt_ref, q_ref, o_ref, u_ref, bias_ref):
    qt = pl.program_id(1)
    tiles_per_step = KV_STEP // TQ_DSA
    for c in range(SEQ // KV_STEP):
        @pl.when(qt // tiles_per_step == c)
        def _(c=c):
            _dsa_body((c + 1) * KV_STEP, qt, ikk_ref, kk_ref, vvt_ref, iq_ref, iwt_ref, q_ref, o_ref,
                      u_ref, bias_ref)


def _dsa_body(nk, qt, ikk_ref, kk_ref, vvt_ref, iq_ref, iwt_ref, q_ref, o_ref, u_ref, bias_ref):
    n_keep = float(B_TOPK)
    key_pos = lax.broadcasted_iota(jnp.int32, (nk, TQ_DSA), 0)
    q_pos = qt * TQ_DSA + lax.broadcasted_iota(jnp.int32, (nk, TQ_DSA), 1)
    causal = key_pos <= q_pos

    iq = iq_ref[...]
    iw = iwt_ref[...]
    ikk = ikk_ref[0:nk, :]
    score = jnp.zeros((nk, TQ_DSA), jnp.float32)
    for h in range(B_IDX_HEADS):
        blk = iq[:, (h // 2) * LANES:(h // 2 + 1) * LANES]
        qh = jnp.where(_head_mask(h % 2), blk, jnp.zeros_like(blk))
        score = score + jnp.maximum(_dot_nt(ikk, qh), 0.0) * iw[h:h + 1, :]

    score = jnp.where(score == 0.0, 0.0, score)
    bits = pltpu.bitcast(score, jnp.int32)
    u = jnp.where(bits < 0, bits ^ jnp.int32(0x7FFFFFFF), bits)
    u_ref[0:nk, :] = jnp.where(causal, u, jnp.int32(INT_MIN))

    def bisect(i, tau):
        trial = tau ^ jnp.left_shift(jnp.int32(1), 31 - i)
        cnt = _colsum(jnp.where(u_ref[0:nk, :] >= trial, 1.0, 0.0))
        return jnp.where(cnt >= n_keep, trial, tau)

    tau = lax.fori_loop(0, 32, bisect, jnp.full((1, TQ_DSA), INT_MIN, jnp.int32))
    bounded = tau > jnp.int32(INT_MIN)

    ge = u_ref[0:nk, :] >= jnp.maximum(tau, jnp.int32(INT_MIN + 1))
    n_ge = _colsum(jnp.where(ge, 1.0, 0.0))
    bias_ref[0:nk, :] = jnp.where(ge, 0.0, NEG_INF)
    excess = jnp.where(bounded, n_ge - n_keep, 0.0)

    @pl.when(jnp.max(excess) > 0.0)
    def _():
        n_gt = _colsum(jnp.where(u_ref[0:nk, :] > tau, 1.0, 0.0))
        need = n_keep - n_gt
        r_i = lax.broadcasted_iota(jnp.int32, (TIE_CHUNK, TIE_CHUNK), 0)
        c_i = lax.broadcasted_iota(jnp.int32, (TIE_CHUNK, TIE_CHUNK), 1)
        below = jnp.where(c_i < r_i, 1.0, 0.0).astype(jnp.bfloat16)
        seen = jnp.zeros((1, TQ_DSA), jnp.float32)
        for c in range(nk // TIE_CHUNK):
            uc = u_ref[c * TIE_CHUNK:(c + 1) * TIE_CHUNK, :]
            eq = jnp.where((uc == tau) & bounded, 1.0, 0.0)
            rank = _dot(below, eq.astype(jnp.bfloat16)) + seen
            seen = seen + jnp.sum(eq, axis=0, keepdims=True)
            keep = (uc > tau) | ((eq > 0.0) & (rank < need))
            bias_ref[c * TIE_CHUNK:(c + 1) * TIE_CHUNK, :] = jnp.where(keep, 0.0, NEG_INF)

    q = q_ref[...]
    kk = kk_ref[0:nk, :]
    vvt = vvt_ref[:, 0:nk]
    bias = bias_ref[0:nk, :]
    outs = []
    for h in range(B_HEADS):
        blk = q[:, (h // 2) * LANES:(h // 2 + 1) * LANES]
        qh = jnp.where(_head_mask(h % 2), blk, jnp.zeros_like(blk))
        s = _dot_nt(kk, qh) + bias
        m = _colmax(s)
        p = jnp.exp(s - m)
        l = _colsum(p)
        o = _dot(vvt, p.astype(jnp.bfloat16)) / l
        outs.append(o[(h % 2) * HEAD_DIM:(h % 2 + 1) * HEAD_DIM])
    o_ref[...] = jnp.concatenate(outs, axis=0).T.astype(o_ref.dtype)


def _dsa(main, vt, iwt, bsz):
    t = main.shape[0]
    n_qt = SEQ // TQ_DSA
    return pl.pallas_call(
        _dsa_kernel,
        grid=(bsz, n_qt),
        in_specs=[
            pl.BlockSpec((SEQ, LANES), lambda b, i: (b, _MAIN_OFF["i_kk"] // LANES)),
            pl.BlockSpec((SEQ, LANES), lambda b, i: (b, _MAIN_OFF["b_kk"] // LANES)),
            pl.BlockSpec((LANES, SEQ), lambda b, i: (A_WIDTH // LANES, b)),
            pl.BlockSpec((TQ_DSA, 2 * LANES), lambda b, i: (b * n_qt + i, _MAIN_OFF["i_q"] // (2 * LANES))),
            pl.BlockSpec((IWT_ROWS, TQ_DSA), lambda b, i: (0, b * n_qt + i)),
            pl.BlockSpec((TQ_DSA, B_WIDTH), lambda b, i: (b * n_qt + i, _MAIN_OFF["b_q"] // B_WIDTH)),
        ],
        out_specs=pl.BlockSpec((TQ_DSA, B_WIDTH), lambda b, i: (b * n_qt + i, 0)),
        out_shape=jax.ShapeDtypeStruct((t, B_WIDTH), jnp.bfloat16),
        scratch_shapes=[
            pltpu.VMEM((SEQ, TQ_DSA), jnp.int32),
            pltpu.VMEM((SEQ, TQ_DSA), jnp.float32),
        ],
        compiler_params=pltpu.CompilerParams(
            dimension_semantics=("parallel", "arbitrary"), vmem_limit_bytes=VMEM_LIMIT),
        name="dsa",
    )(main, main, vt, main, iwt, main)


N_CBLK = SEQ // C_BLOCK
N_GROUPS = len(C_GROUPS)


def _dilated_kernel(*refs):
    q_in = refs[0:N_GROUPS]
    k_in = refs[N_GROUPS:2 * N_GROUPS]
    v_in = refs[2 * N_GROUPS:3 * N_GROUPS]
    o_ref = refs[3 * N_GROUPS]
    scratch = refs[3 * N_GROUPS + 1:]
    tmp_ref = scratch[0]
    dense = scratch[1:1 + 3 * (N_GROUPS - 1)]
    od_ref, ld_ref = scratch[-2 - 2 * N_GROUPS], scratch[-1 - 2 * N_GROUPS]
    o_tok = scratch[-2 * N_GROUPS:-N_GROUPS]
    l_tok = scratch[-N_GROUPS:]

    lane = lax.broadcasted_iota(jnp.int32, (1, LANES), 1)
    low = lane < HEAD_DIM
    qi = lax.broadcasted_iota(jnp.int32, (C_BLOCK, C_BLOCK), 0)
    ki = lax.broadcasted_iota(jnp.int32, (C_BLOCK, C_BLOCK), 1)
    own_ok = ki <= qi

    for g, (window, dil) in enumerate(C_GROUPS):
        assert window // dil == C_BLOCK
        n_sub = SEQ // dil
        n_blk = n_sub // C_BLOCK
        if dil == 1:
            qd, kd, vd = q_in[g], k_in[g], v_in[g]
        else:
            qd, kd, vd = dense[3 * (g - 1):3 * g]
            for src, dst in ((q_in[g], qd), (k_in[g], kd), (v_in[g], vd)):
                tmp_ref[...] = src[...].astype(jnp.float32)
                for r in range(dil):
                    dst[r * n_sub:(r + 1) * n_sub, :] = tmp_ref[pl.ds(r, n_sub, stride=dil), :].astype(dst.dtype)
        o_dst = o_tok[g] if dil == 1 else od_ref
        l_dst = l_tok[g] if dil == 1 else ld_ref

        blocked = (N_CBLK, C_BLOCK, LANES)
        q3 = qd[...].reshape(blocked)
        k3 = kd[...].reshape(blocked)
        v3 = vd[...].reshape(blocked)
        if n_blk > 1:
            pad = jnp.zeros((1, C_BLOCK, LANES), k3.dtype)
            k3 = jnp.concatenate([jnp.concatenate([pad, k3[:-1]], axis=0), k3], axis=1)
            v3 = jnp.concatenate([jnp.concatenate([pad, v3[:-1]], axis=0), v3], axis=1)
            shape = (N_CBLK, C_BLOCK, 2 * C_BLOCK)
            b_i = lax.broadcasted_iota(jnp.int32, shape, 0)
            q_i = lax.broadcasted_iota(jnp.int32, shape, 1)
            k_i = lax.broadcasted_iota(jnp.int32, shape, 2)
            lo = jnp.minimum(q_i + jnp.where((b_i & (n_blk - 1)) == 0, C_BLOCK, 0), C_BLOCK)
            bias = jnp.where(k_i >= lo, jnp.where(k_i <= q_i + C_BLOCK, 0.0, NEG_INF), NEG_INF)
        else:
            bias = jnp.where(own_ok, 0.0, NEG_INF)[None]
        o_h, l_h = [], []
        for hh in range(2):
            qh = jnp.where(_head_mask(hh)[None], q3, jnp.zeros_like(q3))
            s = jnp.einsum("bqd,bkd->bqk", qh, k3, preferred_element_type=jnp.float32) + bias
            m = jnp.max(s, axis=-1, keepdims=True)
            p = jnp.exp(s - m)
            l = jnp.sum(p, axis=-1, keepdims=True)
            o = jnp.einsum("bqk,bkd->bqd", p.astype(jnp.bfloat16), v3, preferred_element_type=jnp.float32)
            o_h.append(o / l)
            l_h.append(m + jnp.log(l))
        o_dst[...] = jnp.where(low[None], o_h[0], o_h[1]).reshape(SEQ, LANES)
        l_dst[...] = jnp.where(low[None], l_h[0], l_h[1]).reshape(SEQ, LANES)
        if dil > 1:
            for r in range(dil):
                o_tok[g][pl.ds(r, n_sub, stride=dil), :] = od_ref[r * n_sub:(r + 1) * n_sub, :]
                l_tok[g][pl.ds(r, n_sub, stride=dil), :] = ld_ref[r * n_sub:(r + 1) * n_sub, :]

    lses = [l_tok[g][...] for g in range(N_GROUPS)]
    top = functools.reduce(jnp.maximum, lses)
    es = [jnp.exp(x - top) for x in lses]
    num = sum(e * o_tok[g][...] for g, e in enumerate(es))
    o_ref[...] = (num / sum(es)).astype(o_ref.dtype)


def _dilated(main, bsz):
    t = main.shape[0]
    n_pairs = C_SLOTS // 2

    def col_spec(name, g):
        base = _MAIN_OFF[name] // LANES + g * n_pairs
        return pl.BlockSpec((SEQ, LANES), lambda b, p: (b, base + p))

    in_specs = ([col_spec("c_q", g) for g in range(N_GROUPS)]
                + [col_spec("c_k", g) for g in range(N_GROUPS)]
                + [col_spec("c_v", g) for g in range(N_GROUPS)])
    scratch = ([pltpu.VMEM((SEQ, LANES), jnp.float32)]
               + [pltpu.VMEM((SEQ, LANES), jnp.bfloat16)] * (3 * (N_GROUPS - 1))
               + [pltpu.VMEM((SEQ, LANES), jnp.float32)] * (2 + 2 * N_GROUPS))
    return pl.pallas_call(
        _dilated_kernel,
        grid=(bsz, n_pairs),
        in_specs=in_specs,
        out_specs=pl.BlockSpec((SEQ, LANES), lambda b, p: (b, p)),
        out_shape=jax.ShapeDtypeStruct((t, C_WIDTH), jnp.bfloat16),
        scratch_shapes=scratch,
        compiler_params=pltpu.CompilerParams(
            dimension_semantics=("parallel", "parallel"), vmem_limit_bytes=VMEM_LIMIT),
        name="dilated",
    )(*([main] * (3 * N_GROUPS)))


def _sigmoid(x):
    return 1.0 / (1.0 + jnp.exp(-x))


def _merge_kernel(x_ref, g_ref, ya_ref, yb_ref, yc_ref, wg_ref, wa_ref, wb_ref, wc_ref, wo_ref, fg_ref,
                  o_ref, *, final):
    x = x_ref[...]
    h = _rms(x, g_ref[...]).astype(jnp.bfloat16)
    merged = jnp.zeros(x.shape, jnp.float32)
    c0 = 0
    m0 = A_WIDTH + B_WIDTH + C_WIDTH
    for i, (y_ref, w_ref) in enumerate(((ya_ref, wa_ref), (yb_ref, wb_ref), (yc_ref, wc_ref))):
        width = y_ref.shape[1]
        gate = _dot(h, wg_ref[:, c0:c0 + width])
        z = (y_ref[...].astype(jnp.float32) * (gate * _sigmoid(gate))).astype(jnp.bfloat16)
        mix = _sigmoid(_dot(h, wg_ref[:, m0 + i * D_MODEL:m0 + (i + 1) * D_MODEL]))
        merged = merged + mix * _dot(z, w_ref[...])
        c0 += width
    out = x + _dot(merged.astype(jnp.bfloat16), wo_ref[...])
    if final:
        out = _rms(out, fg_ref[...])
    o_ref[...] = out


def _merge(x2, g, ya, yb, yc, w_gate, w_a, w_b, w_c, w_o, fg, final):
    t = x2.shape[0]
    row = lambda w: pl.BlockSpec((TM_MERGE, w), lambda i: (i, 0))
    full = lambda a: pl.BlockSpec(a.shape, lambda i: (0, 0))
    return pl.pallas_call(
        functools.partial(_merge_kernel, final=final),
        grid=(t // TM_MERGE,),
        in_specs=[row(D_MODEL), full(g), row(A_WIDTH), row(B_WIDTH), row(C_WIDTH),
                  full(w_gate), full(w_a), full(w_b), full(w_c), full(w_o), full(fg)],
        out_specs=row(D_MODEL),
        out_shape=jax.ShapeDtypeStruct((t, D_MODEL), jnp.float32),
        compiler_params=pltpu.CompilerParams(dimension_semantics=("parallel",),
                                             vmem_limit_bytes=VMEM_LIMIT),
        name="merge_final" if final else "merge",
    )(x2, g, ya, yb, yc, w_gate, w_a, w_b, w_c, w_o, fg)


def _rope_tables():
    inv_freq = 1.0 / (ROPE_THETA ** (jnp.arange(0, HEAD_DIM, 2, dtype=jnp.float32) / HEAD_DIM))
    ang = jnp.arange(SEQ, dtype=jnp.float32)[:, None] * inv_freq[None, :]
    cos, sin = jnp.cos(ang), jnp.sin(ang)
    reps = LANES // HEAD_DIM
    cos_t = jnp.tile(jnp.concatenate([cos, cos], axis=1), (1, reps))
    sin_t = jnp.tile(jnp.concatenate([-sin, sin], axis=1), (1, reps))
    return cos_t, sin_t


def _split_w_in(w):
    names = ("a_q", "a_k", "a_v", "a_g", "b_q", "b_k", "b_v", "b_g", "i_q", "i_k", "i_w",
             "c_q", "c_k", "c_v", "c_g", "m_g")
    return {n: w[:, _OFF[i]:_OFF[i + 1]] for i, n in enumerate(names)}


def _layer_weights(w_in):
    p = _split_w_in(w_in)
    scale = HEAD_DIM ** -0.5
    main = jnp.concatenate([
        p["a_q"] * scale, p["a_k"], p["b_q"] * scale, p["b_k"], p["b_k"],
        p["i_q"], p["i_k"], p["i_k"], p["c_q"] * scale, p["c_k"], p["c_v"]], axis=1)
    iw_scale = (B_IDX_HEADS * B_IDX_DIM) ** -0.5
    w_t = jnp.concatenate([
        p["a_v"].T, p["b_v"].T, p["b_v"].T, p["i_w"].T * iw_scale,
        jnp.zeros((IWT_ROWS - B_IDX_HEADS, D_MODEL), w_in.dtype)], axis=0)
    gates = jnp.concatenate([p["a_g"], p["b_g"], p["c_g"], p["m_g"]], axis=1)
    bf = jnp.bfloat16
    return main.astype(bf), w_t.astype(bf), gates.astype(bf)


def kernel(x, norm_g, w_in, w_br_a, w_br_b, w_br_c, w_out, final_norm_g):
    bsz, seq, d_model = x.shape
    assert seq == SEQ and d_model == D_MODEL
    assert MAIN_WIDTH % LANES == 0 and GATE_WIDTH == sum(IN_SPLITS[i] for i in (3, 7, 14, 15))
    cos_t, sin_t = _rope_tables()
    bf = jnp.bfloat16
    x2 = x.reshape(bsz * seq, d_model)
    fg = final_norm_g.reshape(1, d_model)
    for layer in range(DEPTH):
        w_main, w_t, w_gate = _layer_weights(w_in[layer])
        g = norm_g[layer].reshape(1, d_model)
        main, vt, iwt = _proj(x2, g, w_main, w_t, cos_t, sin_t)
        ya = _moba(main, vt, bsz)
        yb = _dsa(main, vt, iwt, bsz)
        yc = _dilated(main, bsz)
        x2 = _merge(x2, g, ya, yb, yc, w_gate, w_br_a[layer].astype(bf), w_br_b[layer].astype(bf),
                    w_br_c[layer].astype(bf), w_out[layer].astype(bf), fg, final=(layer == DEPTH - 1))
    return x2.reshape(bsz, seq, d_model)
```

```python
import functools

import numpy as np
import jax
import jax.numpy as jnp
from jax import lax
from jax.experimental import pallas as pl
from jax.experimental.pallas import tpu as pltpu

D_MODEL = 1024
SEQ = 2048
DEPTH = 2
HEAD_DIM = 64
ROPE_THETA = 10000.0
RMS_EPS = 1e-6
N_BRANCH = 3

A_HEADS = 6
A_WIDTH = A_HEADS * HEAD_DIM
A_BLOCK = 256
A_TOPK = 3

B_HEADS = 6
B_WIDTH = B_HEADS * HEAD_DIM
B_TOPK = 256
B_IDX_HEADS = 4
B_IDX_DIM = HEAD_DIM

C_GROUPS = ((128, 1), (512, 4), (2048, 16))
C_SLOTS = 4
C_HEADS = C_SLOTS * len(C_GROUPS)
C_WIDTH = C_SLOTS * HEAD_DIM
C_BLOCK = 128

IN_SPLITS = (A_WIDTH, A_WIDTH, A_WIDTH, A_WIDTH,
             B_WIDTH, HEAD_DIM, HEAD_DIM, B_WIDTH, B_IDX_HEADS * B_IDX_DIM, B_IDX_DIM, B_IDX_HEADS,
             C_HEADS * HEAD_DIM, C_HEADS * HEAD_DIM, C_HEADS * HEAD_DIM, C_WIDTH,
             N_BRANCH * D_MODEL)
_OFF = tuple(int(o) for o in np.cumsum((0,) + IN_SPLITS))

LANES = 128
VMEM_LIMIT = 56 * 1024 * 1024

_MAIN_GROUPS = (("a_q", A_WIDTH), ("a_k", A_WIDTH), ("b_q", B_WIDTH), ("b_kk", LANES),
                ("i_q", B_IDX_HEADS * B_IDX_DIM), ("i_kk", LANES),
                ("c_q", C_HEADS * HEAD_DIM), ("c_k", C_HEADS * HEAD_DIM), ("c_v", C_HEADS * HEAD_DIM))
_MAIN_OFF = {}
_c = 0
for _n, _w in _MAIN_GROUPS:
    _MAIN_OFF[_n] = _c
    _c += _w
MAIN_WIDTH = _c
ROPE_WIDTH = _MAIN_OFF["c_v"]
VT_ROWS = A_WIDTH + LANES
IWT_ROWS = 8
GATE_WIDTH = A_WIDTH + B_WIDTH + C_WIDTH + N_BRANCH * D_MODEL

TM_PROJ = 512
TM_MERGE = 512
TQ_DSA = 128
INT_MIN = -2 ** 31
LOG2_E = 1.4426950408889634
NEG_INF = float("-inf")


def _dot(a, b):
    return jnp.dot(a, b, preferred_element_type=jnp.float32)


def _dot_nt(a, b):
    return lax.dot_general(a, b, (((1,), (1,)), ((), ())), preferred_element_type=jnp.float32)


def _colsum(x):
    n = x.shape[0]
    part = x.reshape(n // LANES, LANES, x.shape[1]).sum(axis=0)
    return part.sum(axis=0, keepdims=True)


def _colmax(x):
    n = x.shape[0]
    part = x.reshape(n // LANES, LANES, x.shape[1]).max(axis=0)
    return part.max(axis=0, keepdims=True)


def _rms(x, g):
    return x * lax.rsqrt(jnp.mean(x * x, axis=-1, keepdims=True) + RMS_EPS) * g


def _head_mask(hh):
    lane = lax.broadcasted_iota(jnp.int32, (1, LANES), 1)
    return (lane >= hh * HEAD_DIM) & (lane < (hh + 1) * HEAD_DIM)


def _proj_kernel(x_ref, g_ref, w_ref, wt_ref, cos_ref, sin_ref, main_ref, vt_ref, iwt_ref):
    h = _rms(x_ref[...], g_ref[...]).astype(jnp.bfloat16)
    cos = cos_ref[...]
    sin = sin_ref[...]
    lane = lax.broadcasted_iota(jnp.int32, (1, LANES), 1)
    first_half = (lane % HEAD_DIM) < (HEAD_DIM // 2)
    chunk = 4 * LANES
    for c0 in range(0, MAIN_WIDTH, chunk):
        w = min(chunk, MAIN_WIDTH - c0)
        res = _dot(h, w_ref[:, c0:c0 + w])
        for j in range(0, w, LANES):
            blk = res[:, j:j + LANES]
            if c0 + j < ROPE_WIDTH:
                partner = jnp.where(first_half, pltpu.roll(blk, LANES - HEAD_DIM // 2, 1),
                                    pltpu.roll(blk, HEAD_DIM // 2, 1))
                blk = blk * cos + partner * sin
            main_ref[:, c0 + j:c0 + j + LANES] = blk.astype(main_ref.dtype)
    rt = _dot_nt(wt_ref[...], h)
    vt_ref[...] = rt[:VT_ROWS].astype(vt_ref.dtype)
    iwt_ref[...] = rt[VT_ROWS:]


def _proj(x2, g, w_main, w_t, cos_t, sin_t):
    t = x2.shape[0]
    n_rope_tiles = SEQ // TM_PROJ
    return pl.pallas_call(
        _proj_kernel,
        grid=(t // TM_PROJ,),
        in_specs=[
            pl.BlockSpec((TM_PROJ, D_MODEL), lambda i: (i, 0)),
            pl.BlockSpec((1, D_MODEL), lambda i: (0, 0)),
            pl.BlockSpec((D_MODEL, MAIN_WIDTH), lambda i: (0, 0)),
            pl.BlockSpec((VT_ROWS + IWT_ROWS, D_MODEL), lambda i: (0, 0)),
            pl.BlockSpec((TM_PROJ, LANES), lambda i: (i % n_rope_tiles, 0)),
            pl.BlockSpec((TM_PROJ, LANES), lambda i: (i % n_rope_tiles, 0)),
        ],
        out_specs=[
            pl.BlockSpec((TM_PROJ, MAIN_WIDTH), lambda i: (i, 0)),
            pl.BlockSpec((VT_ROWS, TM_PROJ), lambda i: (0, i)),
            pl.BlockSpec((IWT_ROWS, TM_PROJ), lambda i: (0, i)),
        ],
        out_shape=[
            jax.ShapeDtypeStruct((t, MAIN_WIDTH), jnp.bfloat16),
            jax.ShapeDtypeStruct((VT_ROWS, t), jnp.bfloat16),
            jax.ShapeDtypeStruct((IWT_ROWS, t), jnp.float32),
        ],
        compiler_params=pltpu.CompilerParams(dimension_semantics=("parallel",),
                                             vmem_limit_bytes=VMEM_LIMIT),
        name="proj",
    )(x2, g, w_main, w_t, cos_t, sin_t)


N_ABLK = SEQ // A_BLOCK


def _moba_kernel(q_ref, k_ref, vt_ref, o_ref, kmean_ref):
    qt = pl.program_id(2)

    @pl.when(qt == 0)
    def _():
        for n in range(N_ABLK):
            kb = k_ref[n * A_BLOCK:(n + 1) * A_BLOCK, :].astype(jnp.float32)
            kmean_ref[n:n + 1, :] = jnp.mean(kb, axis=0, keepdims=True)

    for own in range(N_ABLK):
        @pl.when(qt == own)
        def _(own=own):
            _moba_body(own, q_ref, k_ref, vt_ref, o_ref, kmean_ref)


def _moba_body(own, q_ref, k_ref, vt_ref, o_ref, kmean_ref):
    nk = (own + 1) * A_BLOCK
    q = q_ref[...]
    kmean = kmean_ref[...].astype(jnp.bfloat16)
    cols = 2 * A_BLOCK
    blk_f = lax.broadcasted_iota(jnp.int32, (N_ABLK, cols), 0).astype(jnp.float32)
    kpos = lax.broadcasted_iota(jnp.int32, (A_BLOCK, cols), 0)
    qpos = lax.broadcasted_iota(jnp.int32, (A_BLOCK, cols), 1) & (A_BLOCK - 1)
    causal = kpos <= qpos
    q2 = jnp.concatenate([jnp.where(_head_mask(hh), q, jnp.zeros_like(q)) for hh in range(2)], axis=0)
    gate = _dot_nt(kmean, q2)
    gate = jnp.where(blk_f < float(own), gate, NEG_INF)
    sel = jnp.zeros(gate.shape, jnp.float32)
    for _ in range(min(A_TOPK, own)):
        top = jnp.max(gate, axis=0, keepdims=True)
        is_top = (gate == top) & (top > NEG_INF)
        idx = jnp.min(jnp.where(is_top, blk_f, float(N_ABLK)), axis=0, keepdims=True)
        pick = blk_f == idx
        sel = jnp.where(pick, 1.0, sel)
        gate = jnp.where(pick, NEG_INF, gate)
    s = _dot_nt(k_ref[0:nk, :], q2)
    parts = [jnp.where(sel[n:n + 1, :] > 0.0, s[n * A_BLOCK:(n + 1) * A_BLOCK], NEG_INF)
             for n in range(own)]
    parts.append(jnp.where(causal, s[own * A_BLOCK:], NEG_INF))
    s = jnp.concatenate(parts, axis=0)
    m = _colmax(s)
    p = jnp.exp2(s - m)
    l = _colsum(p)
    o = _dot(vt_ref[:, 0:nk], p.astype(jnp.bfloat16)) / l
    o = jnp.concatenate([o[hh * HEAD_DIM:(hh + 1) * HEAD_DIM, hh * A_BLOCK:(hh + 1) * A_BLOCK]
                         for hh in range(2)], axis=0)
    o_ref[...] = o.T.astype(o_ref.dtype)


def _moba(main, vt, bsz):
    t = main.shape[0]
    n_pairs = A_HEADS // 2
    q_col = _MAIN_OFF["a_q"] // LANES
    k_col = _MAIN_OFF["a_k"] // LANES
    return pl.pallas_call(
        _moba_kernel,
        grid=(bsz, n_pairs, N_ABLK),
        in_specs=[
            pl.BlockSpec((A_BLOCK, LANES), lambda b, p, i: (b * N_ABLK + i, q_col + p)),
            pl.BlockSpec((SEQ, LANES), lambda b, p, i: (b, k_col + p)),
            pl.BlockSpec((LANES, SEQ), lambda b, p, i: (p, b)),
        ],
        out_specs=pl.BlockSpec((A_BLOCK, LANES), lambda b, p, i: (b * N_ABLK + i, p)),
        out_shape=jax.ShapeDtypeStruct((t, A_WIDTH), jnp.bfloat16),
        scratch_shapes=[pltpu.VMEM((N_ABLK, LANES), jnp.float32)],
        compiler_params=pltpu.CompilerParams(
            dimension_semantics=("parallel", "parallel", "arbitrary"), vmem_limit_bytes=VMEM_LIMIT),
        name="moba",
    )(main, main, vt)


TIE_CHUNK = 256


KV_STEP = 256


I16_MIN = -2 ** 15


def _bisect16(x_ref, nk, base):
    rows = 2 * LANES

    def step(i, t_off):
        trial_off = t_off | jnp.left_shift(jnp.int32(1), 15 - i)
        trial = trial_off + jnp.int32(I16_MIN)
        word = (trial & jnp.int32(0xFFFF)) | jnp.left_shift(trial, 16)
        trial16 = pltpu.bitcast(jnp.broadcast_to(word, (rows // 2, LANES)), jnp.int16)
        acc = jnp.zeros((rows, LANES), jnp.int16)
        for c in range(nk // rows):
            acc = jnp.where(x_ref[c * rows:(c + 1) * rows, :] >= trial16, acc + jnp.int16(1), acc)
        cnt = base + acc.astype(jnp.int32).sum(axis=0, keepdims=True)
        return jnp.where(cnt >= B_TOPK, trial_off, t_off)

    t_off = lax.fori_loop(0, 16, step, jnp.zeros((1, LANES), jnp.int32))
    return t_off + jnp.int32(I16_MIN)


def _dsa_kernel(ikk_ref, kk_ref, vvt_ref, iq_ref, iwt_ref, q_ref, o_ref, u_ref, h_ref, bias_ref):
    qt = pl.program_id(1)
    tiles_per_step = KV_STEP // TQ_DSA
    for c in range(SEQ // KV_STEP):
        @pl.when(qt // tiles_per_step == c)
        def _(c=c):
            _dsa_body((c + 1) * KV_STEP, qt, ikk_ref, kk_ref, vvt_ref, iq_ref, iwt_ref, q_ref, o_ref,
                      u_ref, h_ref, bias_ref)


def _dsa_body(nk, qt, ikk_ref, kk_ref, vvt_ref, iq_ref, iwt_ref, q_ref, o_ref, u_ref, h_ref, bias_ref):
    n_keep = float(B_TOPK)
    key_pos = lax.broadcasted_iota(jnp.int32, (nk, TQ_DSA), 0)
    q_pos = qt * TQ_DSA + lax.broadcasted_iota(jnp.int32, (nk, TQ_DSA), 1)
    causal = key_pos <= q_pos

    iw = iwt_ref[...]
    iq_all = _stack_heads(iq_ref[...], B_IDX_HEADS)
    iw_all = jnp.concatenate([iw[h:h + 1, :] for h in range(B_IDX_HEADS)], axis=1)
    part = jnp.maximum(_dot_nt(ikk_ref[0:nk, :], iq_all), 0.0) * iw_all
    score = part[:, 0:TQ_DSA]
    for h in range(1, B_IDX_HEADS):
        score = score + part[:, h * TQ_DSA:(h + 1) * TQ_DSA]

    score = jnp.where(score == 0.0, 0.0, score)
    bits = pltpu.bitcast(score, jnp.int32)
    u = jnp.where(bits < 0, bits ^ jnp.int32(0x7FFFFFFF), bits)
    u = jnp.where(causal, u, jnp.int32(INT_MIN))
    u_ref[0:nk, :] = u

    u_hi = u >> 16
    h_ref[0:nk, :] = u_hi.astype(jnp.int16)
    zero = jnp.zeros((1, TQ_DSA), jnp.int32)
    t_hi = _bisect16(h_ref, nk, zero)
    n_above = _colsum(jnp.where(u_hi > t_hi, 1.0, 0.0)).astype(jnp.int32)
    low = (u & jnp.int32(0xFFFF)) + jnp.int32(I16_MIN)
    h_ref[0:nk, :] = jnp.where(u_hi == t_hi, low, jnp.int32(I16_MIN)).astype(jnp.int16)
    t_lo = _bisect16(h_ref, nk, n_above)
    tau = t_hi * jnp.int32(1 << 16) + (t_lo - jnp.int32(I16_MIN))
    bounded = tau > jnp.int32(INT_MIN)

    ge = u_ref[0:nk, :] >= jnp.maximum(tau, jnp.int32(INT_MIN + 1))
    n_ge = _colsum(jnp.where(ge, 1.0, 0.0))
    bias_ref[0:nk, :] = jnp.where(ge, 0.0, NEG_INF)
    excess = jnp.where(bounded, n_ge - n_keep, 0.0)

    @pl.when(jnp.max(excess) > 0.0)
    def _():
        n_gt = _colsum(jnp.where(u_ref[0:nk, :] > tau, 1.0, 0.0))
        need = n_keep - n_gt
        r_i = lax.broadcasted_iota(jnp.int32, (TIE_CHUNK, TIE_CHUNK), 0)
        c_i = lax.broadcasted_iota(jnp.int32, (TIE_CHUNK, TIE_CHUNK), 1)
        below = jnp.where(c_i < r_i, 1.0, 0.0).astype(jnp.bfloat16)
        seen = jnp.zeros((1, TQ_DSA), jnp.float32)
        for c in range(nk // TIE_CHUNK):
            uc = u_ref[c * TIE_CHUNK:(c + 1) * TIE_CHUNK, :]
            eq = jnp.where((uc == tau) & bounded, 1.0, 0.0)
            rank = _dot(below, eq.astype(jnp.bfloat16)) + seen
            seen = seen + jnp.sum(eq, axis=0, keepdims=True)
            keep = (uc > tau) | ((eq > 0.0) & (rank < need))
            bias_ref[c * TIE_CHUNK:(c + 1) * TIE_CHUNK, :] = jnp.where(keep, 0.0, NEG_INF)

    q_all = _stack_heads(q_ref[...], B_HEADS)
    bias = bias_ref[0:nk, :]
    s = _dot_nt(kk_ref[0:nk, :], q_all) + jnp.concatenate([bias] * B_HEADS, axis=1)
    m = _colmax(s)
    p = jnp.exp2(s - m)
    l = _colsum(p)
    o = _dot(vvt_ref[:, 0:nk], p.astype(jnp.bfloat16)) / l
    o = jnp.concatenate([o[(h % 2) * HEAD_DIM:(h % 2 + 1) * HEAD_DIM, h * TQ_DSA:(h + 1) * TQ_DSA]
                         for h in range(B_HEADS)], axis=0)
    o_ref[...] = o.T.astype(o_ref.dtype)


def _stack_heads(x, n_heads):
    rows = []
    for h in range(n_heads):
        blk = x[:, (h // 2) * LANES:(h // 2 + 1) * LANES]
        rows.append(jnp.where(_head_mask(h % 2), blk, jnp.zeros_like(blk)))
    return jnp.concatenate(rows, axis=0)


def _dsa(main, vt, iwt, bsz):
    t = main.shape[0]
    n_qt = SEQ // TQ_DSA
    return pl.pallas_call(
        _dsa_kernel,
        grid=(bsz, n_qt),
        in_specs=[
            pl.BlockSpec((SEQ, LANES), lambda b, i: (b, _MAIN_OFF["i_kk"] // LANES)),
            pl.BlockSpec((SEQ, LANES), lambda b, i: (b, _MAIN_OFF["b_kk"] // LANES)),
            pl.BlockSpec((LANES, SEQ), lambda b, i: (A_WIDTH // LANES, b)),
            pl.BlockSpec((TQ_DSA, 2 * LANES), lambda b, i: (b * n_qt + i, _MAIN_OFF["i_q"] // (2 * LANES))),
            pl.BlockSpec((IWT_ROWS, TQ_DSA), lambda b, i: (0, b * n_qt + i)),
            pl.BlockSpec((TQ_DSA, B_WIDTH), lambda b, i: (b * n_qt + i, _MAIN_OFF["b_q"] // B_WIDTH)),
        ],
        out_specs=pl.BlockSpec((TQ_DSA, B_WIDTH), lambda b, i: (b * n_qt + i, 0)),
        out_shape=jax.ShapeDtypeStruct((t, B_WIDTH), jnp.bfloat16),
        scratch_shapes=[
            pltpu.VMEM((SEQ, TQ_DSA), jnp.int32),
            pltpu.VMEM((SEQ, TQ_DSA), jnp.int16),
            pltpu.VMEM((SEQ, TQ_DSA), jnp.float32),
        ],
        compiler_params=pltpu.CompilerParams(
            dimension_semantics=("parallel", "arbitrary"), vmem_limit_bytes=VMEM_LIMIT),
        name="dsa",
    )(main, main, vt, main, iwt, main)


N_CBLK = SEQ // C_BLOCK
N_GROUPS = len(C_GROUPS)


def _dilated_kernel(*refs):
    q_in = refs[0:N_GROUPS]
    k_in = refs[N_GROUPS:2 * N_GROUPS]
    v_in = refs[2 * N_GROUPS:3 * N_GROUPS]
    o_ref = refs[3 * N_GROUPS]
    scratch = refs[3 * N_GROUPS + 1:]
    tmp_ref = scratch[0]
    dense = scratch[1:1 + 3 * (N_GROUPS - 1)]
    od_ref, ld_ref = scratch[-2 - 2 * N_GROUPS], scratch[-1 - 2 * N_GROUPS]
    o_tok = scratch[-2 * N_GROUPS:-N_GROUPS]
    l_tok = scratch[-N_GROUPS:]

    lane = lax.broadcasted_iota(jnp.int32, (1, LANES), 1)
    low = lane < HEAD_DIM
    qi = lax.broadcasted_iota(jnp.int32, (C_BLOCK, C_BLOCK), 0)
    ki = lax.broadcasted_iota(jnp.int32, (C_BLOCK, C_BLOCK), 1)
    own_ok = ki <= qi

    for g, (window, dil) in enumerate(C_GROUPS):
        assert window // dil == C_BLOCK
        n_sub = SEQ // dil
        n_blk = n_sub // C_BLOCK
        if dil == 1:
            qd, kd, vd = q_in[g], k_in[g], v_in[g]
        else:
            qd, kd, vd = dense[3 * (g - 1):3 * g]
            for src, dst in ((q_in[g], qd), (k_in[g], kd), (v_in[g], vd)):
                tmp_ref[...] = src[...].astype(jnp.float32)
                for r in range(dil):
                    dst[r * n_sub:(r + 1) * n_sub, :] = tmp_ref[pl.ds(r, n_sub, stride=dil), :].astype(dst.dtype)
        o_dst = o_tok[g] if dil == 1 else od_ref
        l_dst = l_tok[g] if dil == 1 else ld_ref

        blocked = (N_CBLK, C_BLOCK, LANES)
        q3 = qd[...].reshape(blocked)
        k3 = kd[...].reshape(blocked)
        v3 = vd[...].reshape(blocked)
        if n_blk > 1:
            pad = jnp.zeros((1, C_BLOCK, LANES), k3.dtype)
            k3 = jnp.concatenate([jnp.concatenate([pad, k3[:-1]], axis=0), k3], axis=1)
            v3 = jnp.concatenate([jnp.concatenate([pad, v3[:-1]], axis=0), v3], axis=1)
            shape = (N_CBLK, C_BLOCK, 2 * C_BLOCK)
            b_i = lax.broadcasted_iota(jnp.int32, shape, 0)
            q_i = lax.broadcasted_iota(jnp.int32, shape, 1)
            k_i = lax.broadcasted_iota(jnp.int32, shape, 2)
            lo = jnp.minimum(q_i + jnp.where((b_i & (n_blk - 1)) == 0, C_BLOCK, 0), C_BLOCK)
            bias = jnp.where(k_i >= lo, jnp.where(k_i <= q_i + C_BLOCK, 0.0, NEG_INF), NEG_INF)
        else:
            bias = jnp.where(own_ok, 0.0, NEG_INF)[None]
        o_h, l_h = [], []
        for hh in range(2):
            qh = jnp.where(_head_mask(hh)[None], q3, jnp.zeros_like(q3))
            s = jnp.einsum("bqd,bkd->bqk", qh, k3, preferred_element_type=jnp.float32) + bias
            m = jnp.max(s, axis=-1, keepdims=True)
            p = jnp.exp2(s - m)
            l = jnp.sum(p, axis=-1, keepdims=True)
            o = jnp.einsum("bqk,bkd->bqd", p.astype(jnp.bfloat16), v3, preferred_element_type=jnp.float32)
            o_h.append(o / l)
            l_h.append(m + jnp.log2(l))
        o_dst[...] = jnp.where(low[None], o_h[0], o_h[1]).reshape(SEQ, LANES)
        l_dst[...] = jnp.where(low[None], l_h[0], l_h[1]).reshape(SEQ, LANES)
        if dil > 1:
            for r in range(dil):
                o_tok[g][pl.ds(r, n_sub, stride=dil), :] = od_ref[r * n_sub:(r + 1) * n_sub, :]
                l_tok[g][pl.ds(r, n_sub, stride=dil), :] = ld_ref[r * n_sub:(r + 1) * n_sub, :]

    lses = [l_tok[g][...] for g in range(N_GROUPS)]
    top = functools.reduce(jnp.maximum, lses)
    es = [jnp.exp2(x - top) for x in lses]
    num = sum(e * o_tok[g][...] for g, e in enumerate(es))
    o_ref[...] = (num / sum(es)).astype(o_ref.dtype)


def _dilated(main, bsz):
    t = main.shape[0]
    n_pairs = C_SLOTS // 2

    def col_spec(name, g):
        base = _MAIN_OFF[name] // LANES + g * n_pairs
        return pl.BlockSpec((SEQ, LANES), lambda b, p: (b, base + p))

    in_specs = ([col_spec("c_q", g) for g in range(N_GROUPS)]
                + [col_spec("c_k", g) for g in range(N_GROUPS)]
                + [col_spec("c_v", g) for g in range(N_GROUPS)])
    scratch = ([pltpu.VMEM((SEQ, LANES), jnp.float32)]
               + [pltpu.VMEM((SEQ, LANES), jnp.bfloat16)] * (3 * (N_GROUPS - 1))
               + [pltpu.VMEM((SEQ, LANES), jnp.float32)] * (2 + 2 * N_GROUPS))
    return pl.pallas_call(
        _dilated_kernel,
        grid=(bsz, n_pairs),
        in_specs=in_specs,
        out_specs=pl.BlockSpec((SEQ, LANES), lambda b, p: (b, p)),
        out_shape=jax.ShapeDtypeStruct((t, C_WIDTH), jnp.bfloat16),
        scratch_shapes=scratch,
        compiler_params=pltpu.CompilerParams(
            dimension_semantics=("parallel", "parallel"), vmem_limit_bytes=VMEM_LIMIT),
        name="dilated",
    )(*([main] * (3 * N_GROUPS)))


def _sigmoid(x):
    return 1.0 / (1.0 + jnp.exp(-x))


def _merge_kernel(x_ref, g_ref, ya_ref, yb_ref, yc_ref, wg_ref, wa_ref, wb_ref, wc_ref, wo_ref, fg_ref,
                  o_ref, *, final):
    x = x_ref[...]
    h = _rms(x, g_ref[...]).astype(jnp.bfloat16)
    merged = jnp.zeros(x.shape, jnp.float32)
    c0 = 0
    m0 = A_WIDTH + B_WIDTH + C_WIDTH
    for i, (y_ref, w_ref) in enumerate(((ya_ref, wa_ref), (yb_ref, wb_ref), (yc_ref, wc_ref))):
        width = y_ref.shape[1]
        gate = _dot(h, wg_ref[:, c0:c0 + width])
        z = (y_ref[...].astype(jnp.float32) * (gate * _sigmoid(gate))).astype(jnp.bfloat16)
        mix = _sigmoid(_dot(h, wg_ref[:, m0 + i * D_MODEL:m0 + (i + 1) * D_MODEL]))
        merged = merged + mix * _dot(z, w_ref[...])
        c0 += width
    out = x + _dot(merged.astype(jnp.bfloat16), wo_ref[...])
    if final:
        out = _rms(out, fg_ref[...])
    o_ref[...] = out


def _merge(x2, g, ya, yb, yc, w_gate, w_a, w_b, w_c, w_o, fg, final):
    t = x2.shape[0]
    row = lambda w: pl.BlockSpec((TM_MERGE, w), lambda i: (i, 0))
    full = lambda a: pl.BlockSpec(a.shape, lambda i: (0, 0))
    return pl.pallas_call(
        functools.partial(_merge_kernel, final=final),
        grid=(t // TM_MERGE,),
        in_specs=[row(D_MODEL), full(g), row(A_WIDTH), row(B_WIDTH), row(C_WIDTH),
                  full(w_gate), full(w_a), full(w_b), full(w_c), full(w_o), full(fg)],
        out_specs=row(D_MODEL),
        out_shape=jax.ShapeDtypeStruct((t, D_MODEL), jnp.float32),
        compiler_params=pltpu.CompilerParams(dimension_semantics=("parallel",),
                                             vmem_limit_bytes=VMEM_LIMIT),
        name="merge_final" if final else "merge",
    )(x2, g, ya, yb, yc, w_gate, w_a, w_b, w_c, w_o, fg)


def _rope_tables():
    inv_freq = 1.0 / (ROPE_THETA ** (jnp.arange(0, HEAD_DIM, 2, dtype=jnp.float32) / HEAD_DIM))
    ang = jnp.arange(SEQ, dtype=jnp.float32)[:, None] * inv_freq[None, :]
    cos, sin = jnp.cos(ang), jnp.sin(ang)
    reps = LANES // HEAD_DIM
    cos_t = jnp.tile(jnp.concatenate([cos, cos], axis=1), (1, reps))
    sin_t = jnp.tile(jnp.concatenate([-sin, sin], axis=1), (1, reps))
    return cos_t, sin_t


def _split_w_in(w):
    names = ("a_q", "a_k", "a_v", "a_g", "b_q", "b_k", "b_v", "b_g", "i_q", "i_k", "i_w",
             "c_q", "c_k", "c_v", "c_g", "m_g")
    return {n: w[:, _OFF[i]:_OFF[i + 1]] for i, n in enumerate(names)}


def _layer_weights(w_in):
    p = _split_w_in(w_in)
    scale = HEAD_DIM ** -0.5 * LOG2_E
    main = jnp.concatenate([
        p["a_q"] * scale, p["a_k"], p["b_q"] * scale, p["b_k"], p["b_k"],
        p["i_q"], p["i_k"], p["i_k"], p["c_q"] * scale, p["c_k"], p["c_v"]], axis=1)
    iw_scale = (B_IDX_HEADS * B_IDX_DIM) ** -0.5
    w_t = jnp.concatenate([
        p["a_v"].T, p["b_v"].T, p["b_v"].T, p["i_w"].T * iw_scale,
        jnp.zeros((IWT_ROWS - B_IDX_HEADS, D_MODEL), w_in.dtype)], axis=0)
    gates = jnp.concatenate([p["a_g"], p["b_g"], p["c_g"], p["m_g"]], axis=1)
    bf = jnp.bfloat16
    return main.astype(bf), w_t.astype(bf), gates.astype(bf)


def kernel(x, norm_g, w_in, w_br_a, w_br_b, w_br_c, w_out, final_norm_g):
    bsz, seq, d_model = x.shape
    assert seq == SEQ and d_model == D_MODEL
    assert MAIN_WIDTH % LANES == 0 and GATE_WIDTH == sum(IN_SPLITS[i] for i in (3, 7, 14, 15))
    cos_t, sin_t = _rope_tables()
    bf = jnp.bfloat16
    x2 = x.reshape(bsz * seq, d_model)
    fg = final_norm_g.reshape(1, d_model)
    for layer in range(DEPTH):
        w_main, w_t, w_gate = _layer_weights(w_in[layer])
        g = norm_g[layer].reshape(1, d_model)
        main, vt, iwt = _proj(x2, g, w_main, w_t, cos_t, sin_t)
        ya = _moba(main, vt, bsz)
        yb = _dsa(main, vt, iwt, bsz)
        yc = _dilated(main, bsz)
        x2 = _merge(x2, g, ya, yb, yc, w_gate, w_br_a[layer].astype(bf), w_br_b[layer].astype(bf),
                    w_br_c[layer].astype(bf), w_out[layer].astype(bf), fg, final=(layer == DEPTH - 1))
    return x2.reshape(bsz, seq, d_model)
```

```python
import functools

import numpy as np
import jax
import jax.numpy as jnp
from jax import lax
from jax.experimental import pallas as pl
from jax.experimental.pallas import tpu as pltpu

D_MODEL = 1024
SEQ = 2048
DEPTH = 2
HEAD_DIM = 64
ROPE_THETA = 10000.0
RMS_EPS = 1e-6
N_BRANCH = 3

A_HEADS = 6
A_WIDTH = A_HEADS * HEAD_DIM
A_BLOCK = 256
A_TOPK = 3

B_HEADS = 6
B_WIDTH = B_HEADS * HEAD_DIM
B_TOPK = 256
B_IDX_HEADS = 4
B_IDX_DIM = HEAD_DIM

C_GROUPS = ((128, 1), (512, 4), (2048, 16))
C_SLOTS = 4
C_HEADS = C_SLOTS * len(C_GROUPS)
C_WIDTH = C_SLOTS * HEAD_DIM
C_BLOCK = 128

IN_SPLITS = (A_WIDTH, A_WIDTH, A_WIDTH, A_WIDTH,
             B_WIDTH, HEAD_DIM, HEAD_DIM, B_WIDTH, B_IDX_HEADS * B_IDX_DIM, B_IDX_DIM, B_IDX_HEADS,
             C_HEADS * HEAD_DIM, C_HEADS * HEAD_DIM, C_HEADS * HEAD_DIM, C_WIDTH,
             N_BRANCH * D_MODEL)
_OFF = tuple(int(o) for o in np.cumsum((0,) + IN_SPLITS))

LANES = 128
VMEM_LIMIT = 56 * 1024 * 1024

_MAIN_GROUPS = (("a_q", A_WIDTH), ("a_k", A_WIDTH), ("b_q", B_WIDTH), ("b_kk", LANES),
                ("i_q", B_IDX_HEADS * B_IDX_DIM), ("i_kk", LANES),
                ("c_q", C_HEADS * HEAD_DIM), ("c_k", C_HEADS * HEAD_DIM), ("c_v", C_HEADS * HEAD_DIM))
_MAIN_OFF = {}
_c = 0
for _n, _w in _MAIN_GROUPS:
    _MAIN_OFF[_n] = _c
    _c += _w
MAIN_WIDTH = _c
ROPE_WIDTH = _MAIN_OFF["c_v"]
VT_ROWS = A_WIDTH + LANES
IWT_ROWS = 8
GATE_WIDTH = A_WIDTH + B_WIDTH + C_WIDTH + N_BRANCH * D_MODEL

TM_PROJ = 512
TM_MERGE = 512
TQ_DSA = 128
INT_MIN = -2 ** 31
LOG2_E = 1.4426950408889634
NEG_INF = float("-inf")


def _dot(a, b):
    return jnp.dot(a, b, preferred_element_type=jnp.float32)


def _dot_nt(a, b):
    return lax.dot_general(a, b, (((1,), (1,)), ((), ())), preferred_element_type=jnp.float32)


def _colsum(x):
    n = x.shape[0]
    part = x.reshape(n // LANES, LANES, x.shape[1]).sum(axis=0)
    return part.sum(axis=0, keepdims=True)


def _colmax(x):
    n = x.shape[0]
    part = x.reshape(n // LANES, LANES, x.shape[1]).max(axis=0)
    return part.max(axis=0, keepdims=True)


def _rms(x, g):
    return x * lax.rsqrt(jnp.mean(x * x, axis=-1, keepdims=True) + RMS_EPS) * g


def _head_mask(hh):
    lane = lax.broadcasted_iota(jnp.int32, (1, LANES), 1)
    return (lane >= hh * HEAD_DIM) & (lane < (hh + 1) * HEAD_DIM)


def _proj_kernel(x_ref, g_ref, w_ref, wt_ref, cos_ref, sin_ref, main_ref, vt_ref, iwt_ref):
    h = _rms(x_ref[...], g_ref[...]).astype(jnp.bfloat16)
    cos = cos_ref[...]
    sin = sin_ref[...]
    lane = lax.broadcasted_iota(jnp.int32, (1, LANES), 1)
    first_half = (lane % HEAD_DIM) < (HEAD_DIM // 2)
    chunk = 4 * LANES
    for c0 in range(0, MAIN_WIDTH, chunk):
        w = min(chunk, MAIN_WIDTH - c0)
        res = _dot(h, w_ref[:, c0:c0 + w])
        for j in range(0, w, LANES):
            blk = res[:, j:j + LANES]
            if c0 + j < ROPE_WIDTH:
                partner = jnp.where(first_half, pltpu.roll(blk, LANES - HEAD_DIM // 2, 1),
                                    pltpu.roll(blk, HEAD_DIM // 2, 1))
                blk = blk * cos + partner * sin
            main_ref[:, c0 + j:c0 + j + LANES] = blk.astype(main_ref.dtype)
    rt = _dot_nt(wt_ref[...], h)
    vt_ref[...] = rt[:VT_ROWS].astype(vt_ref.dtype)
    iwt_ref[...] = rt[VT_ROWS:]


def _proj(x2, g, w_main, w_t, cos_t, sin_t):
    t = x2.shape[0]
    n_rope_tiles = SEQ // TM_PROJ
    return pl.pallas_call(
        _proj_kernel,
        grid=(t // TM_PROJ,),
        in_specs=[
            pl.BlockSpec((TM_PROJ, D_MODEL), lambda i: (i, 0)),
            pl.BlockSpec((1, D_MODEL), lambda i: (0, 0)),
            pl.BlockSpec((D_MODEL, MAIN_WIDTH), lambda i: (0, 0)),
            pl.BlockSpec((VT_ROWS + IWT_ROWS, D_MODEL), lambda i: (0, 0)),
            pl.BlockSpec((TM_PROJ, LANES), lambda i: (i % n_rope_tiles, 0)),
            pl.BlockSpec((TM_PROJ, LANES), lambda i: (i % n_rope_tiles, 0)),
        ],
        out_specs=[
            pl.BlockSpec((TM_PROJ, MAIN_WIDTH), lambda i: (i, 0)),
            pl.BlockSpec((VT_ROWS, TM_PROJ), lambda i: (0, i)),
            pl.BlockSpec((IWT_ROWS, TM_PROJ), lambda i: (0, i)),
        ],
        out_shape=[
            jax.ShapeDtypeStruct((t, MAIN_WIDTH), jnp.bfloat16),
            jax.ShapeDtypeStruct((VT_ROWS, t), jnp.bfloat16),
            jax.ShapeDtypeStruct((IWT_ROWS, t), jnp.float32),
        ],
        compiler_params=pltpu.CompilerParams(dimension_semantics=("parallel",),
                                             vmem_limit_bytes=VMEM_LIMIT),
        name="proj",
    )(x2, g, w_main, w_t, cos_t, sin_t)


N_ABLK = SEQ // A_BLOCK


def _moba_kernel(q_ref, k_ref, vt_ref, o_ref, kmean_ref):
    qt = pl.program_id(2)

    @pl.when(qt == 0)
    def _():
        for n in range(N_ABLK):
            kb = k_ref[n * A_BLOCK:(n + 1) * A_BLOCK, :].astype(jnp.float32)
            kmean_ref[n:n + 1, :] = jnp.mean(kb, axis=0, keepdims=True)

    for own in range(N_ABLK):
        @pl.when(qt == own)
        def _(own=own):
            _moba_body(own, q_ref, k_ref, vt_ref, o_ref, kmean_ref)


def _moba_body(own, q_ref, k_ref, vt_ref, o_ref, kmean_ref):
    nk = (own + 1) * A_BLOCK
    q = q_ref[...]
    kmean = kmean_ref[...].astype(jnp.bfloat16)
    cols = 2 * A_BLOCK
    blk_f = lax.broadcasted_iota(jnp.int32, (N_ABLK, cols), 0).astype(jnp.float32)
    kpos = lax.broadcasted_iota(jnp.int32, (A_BLOCK, cols), 0)
    qpos = lax.broadcasted_iota(jnp.int32, (A_BLOCK, cols), 1) & (A_BLOCK - 1)
    causal = kpos <= qpos
    q2 = jnp.concatenate([jnp.where(_head_mask(hh), q, jnp.zeros_like(q)) for hh in range(2)], axis=0)
    gate = _dot_nt(kmean, q2)
    gate = jnp.where(blk_f < float(own), gate, NEG_INF)
    sel = jnp.zeros(gate.shape, jnp.float32)
    for _ in range(min(A_TOPK, own)):
        top = jnp.max(gate, axis=0, keepdims=True)
        is_top = (gate == top) & (top > NEG_INF)
        idx = jnp.min(jnp.where(is_top, blk_f, float(N_ABLK)), axis=0, keepdims=True)
        pick = blk_f == idx
        sel = jnp.where(pick, 1.0, sel)
        gate = jnp.where(pick, NEG_INF, gate)
    s = _dot_nt(k_ref[0:nk, :], q2)
    parts = [jnp.where(sel[n:n + 1, :] > 0.0, s[n * A_BLOCK:(n + 1) * A_BLOCK], NEG_INF)
             for n in range(own)]
    parts.append(jnp.where(causal, s[own * A_BLOCK:], NEG_INF))
    s = jnp.concatenate(parts, axis=0)
    m = _colmax(s)
    p = jnp.exp2(s - m)
    l = _colsum(p)
    o = _dot(vt_ref[:, 0:nk], p.astype(jnp.bfloat16)) / l
    o = jnp.concatenate([o[hh * HEAD_DIM:(hh + 1) * HEAD_DIM, hh * A_BLOCK:(hh + 1) * A_BLOCK]
                         for hh in range(2)], axis=0)
    o_ref[...] = o.T.astype(o_ref.dtype)


def _moba(main, vt, bsz):
    t = main.shape[0]
    n_pairs = A_HEADS // 2
    q_col = _MAIN_OFF["a_q"] // LANES
    k_col = _MAIN_OFF["a_k"] // LANES
    return pl.pallas_call(
        _moba_kernel,
        grid=(bsz, n_pairs, N_ABLK),
        in_specs=[
            pl.BlockSpec((A_BLOCK, LANES), lambda b, p, i: (b * N_ABLK + i, q_col + p)),
            pl.BlockSpec((SEQ, LANES), lambda b, p, i: (b, k_col + p)),
            pl.BlockSpec((LANES, SEQ), lambda b, p, i: (p, b)),
        ],
        out_specs=pl.BlockSpec((A_BLOCK, LANES), lambda b, p, i: (b * N_ABLK + i, p)),
        out_shape=jax.ShapeDtypeStruct((t, A_WIDTH), jnp.bfloat16),
        scratch_shapes=[pltpu.VMEM((N_ABLK, LANES), jnp.float32)],
        compiler_params=pltpu.CompilerParams(
            dimension_semantics=("parallel", "parallel", "arbitrary"), vmem_limit_bytes=VMEM_LIMIT),
        name="moba",
    )(main, main, vt)


TIE_CHUNK = 256


KV_STEP = 512


I16_MIN = -2 ** 15


def _bisect16(x_ref, nk, base):
    rows = 2 * LANES

    def step(i, t_off):
        trial_off = t_off | jnp.left_shift(jnp.int32(1), 15 - i)
        trial = trial_off + jnp.int32(I16_MIN)
        word = (trial & jnp.int32(0xFFFF)) | jnp.left_shift(trial, 16)
        trial16 = pltpu.bitcast(jnp.broadcast_to(word, (rows // 2, LANES)), jnp.int16)
        acc = jnp.zeros((rows, LANES), jnp.int16)
        for c in range(nk // rows):
            acc = jnp.where(x_ref[c * rows:(c + 1) * rows, :] >= trial16, acc + jnp.int16(1), acc)
        cnt = base + acc.astype(jnp.int32).sum(axis=0, keepdims=True)
        return jnp.where(cnt >= B_TOPK, trial_off, t_off)

    t_off = lax.fori_loop(0, 16, step, jnp.zeros((1, LANES), jnp.int32))
    return t_off + jnp.int32(I16_MIN)


def _dsa_kernel(ikk_ref, kk_ref, vvt_ref, iq_ref, iwt_ref, q_ref, o_ref, u_ref, h_ref, bias_ref):
    qt = pl.program_id(1)
    tiles_per_step = KV_STEP // TQ_DSA
    for c in range(SEQ // KV_STEP):
        @pl.when(qt // tiles_per_step == c)
        def _(c=c):
            _dsa_body((c + 1) * KV_STEP, qt, ikk_ref, kk_ref, vvt_ref, iq_ref, iwt_ref, q_ref, o_ref,
                      u_ref, h_ref, bias_ref)


def _dsa_body(nk, qt, ikk_ref, kk_ref, vvt_ref, iq_ref, iwt_ref, q_ref, o_ref, u_ref, h_ref, bias_ref):
    n_keep = float(B_TOPK)
    key_pos = lax.broadcasted_iota(jnp.int32, (nk, TQ_DSA), 0)
    q_pos = qt * TQ_DSA + lax.broadcasted_iota(jnp.int32, (nk, TQ_DSA), 1)
    causal = key_pos <= q_pos

    iw = iwt_ref[...]
    iq_all = _stack_heads(iq_ref[...], B_IDX_HEADS)
    iw_all = jnp.concatenate([iw[h:h + 1, :] for h in range(B_IDX_HEADS)], axis=1)
    part = jnp.maximum(_dot_nt(ikk_ref[0:nk, :], iq_all), 0.0) * iw_all
    score = part[:, 0:TQ_DSA]
    for h in range(1, B_IDX_HEADS):
        score = score + part[:, h * TQ_DSA:(h + 1) * TQ_DSA]

    score = jnp.where(score == 0.0, 0.0, score)
    bits = pltpu.bitcast(score, jnp.int32)
    u = jnp.where(bits < 0, bits ^ jnp.int32(0x7FFFFFFF), bits)
    u = jnp.where(causal, u, jnp.int32(INT_MIN))
    u_ref[0:nk, :] = u

    u_hi = u >> 16
    h_ref[0:nk, :] = u_hi.astype(jnp.int16)
    zero = jnp.zeros((1, TQ_DSA), jnp.int32)
    t_hi = _bisect16(h_ref, nk, zero)
    n_above = _colsum(jnp.where(u_hi > t_hi, 1.0, 0.0)).astype(jnp.int32)
    low = (u & jnp.int32(0xFFFF)) + jnp.int32(I16_MIN)
    h_ref[0:nk, :] = jnp.where(u_hi == t_hi, low, jnp.int32(I16_MIN)).astype(jnp.int16)
    t_lo = _bisect16(h_ref, nk, n_above)
    tau = t_hi * jnp.int32(1 << 16) + (t_lo - jnp.int32(I16_MIN))
    bounded = tau > jnp.int32(INT_MIN)

    ge = u_ref[0:nk, :] >= jnp.maximum(tau, jnp.int32(INT_MIN + 1))
    n_ge = _colsum(jnp.where(ge, 1.0, 0.0))
    bias_ref[0:nk, :] = jnp.where(ge, 0.0, NEG_INF)
    excess = jnp.where(bounded, n_ge - n_keep, 0.0)

    @pl.when(jnp.max(excess) > 0.0)
    def _():
        n_gt = _colsum(jnp.where(u_ref[0:nk, :] > tau, 1.0, 0.0))
        need = n_keep - n_gt
        r_i = lax.broadcasted_iota(jnp.int32, (TIE_CHUNK, TIE_CHUNK), 0)
        c_i = lax.broadcasted_iota(jnp.int32, (TIE_CHUNK, TIE_CHUNK), 1)
        below = jnp.where(c_i < r_i, 1.0, 0.0).astype(jnp.bfloat16)
        seen = jnp.zeros((1, TQ_DSA), jnp.float32)
        for c in range(nk // TIE_CHUNK):
            uc = u_ref[c * TIE_CHUNK:(c + 1) * TIE_CHUNK, :]
            eq = jnp.where((uc == tau) & bounded, 1.0, 0.0)
            rank = _dot(below, eq.astype(jnp.bfloat16)) + seen
            seen = seen + jnp.sum(eq, axis=0, keepdims=True)
            keep = (uc > tau) | ((eq > 0.0) & (rank < need))
            bias_ref[c * TIE_CHUNK:(c + 1) * TIE_CHUNK, :] = jnp.where(keep, 0.0, NEG_INF)

    q_all = _stack_heads(q_ref[...], B_HEADS)
    bias = bias_ref[0:nk, :]
    s = _dot_nt(kk_ref[0:nk, :], q_all) + jnp.concatenate([bias] * B_HEADS, axis=1)
    m = _colmax(s)
    p = jnp.exp2(s - m)
    l = _colsum(p)
    o = _dot(vvt_ref[:, 0:nk], p.astype(jnp.bfloat16)) / l
    o = jnp.concatenate([o[(h % 2) * HEAD_DIM:(h % 2 + 1) * HEAD_DIM, h * TQ_DSA:(h + 1) * TQ_DSA]
                         for h in range(B_HEADS)], axis=0)
    o_ref[...] = o.T.astype(o_ref.dtype)


def _stack_heads(x, n_heads):
    rows = []
    for h in range(n_heads):
        blk = x[:, (h // 2) * LANES:(h // 2 + 1) * LANES]
        rows.append(jnp.where(_head_mask(h % 2), blk, jnp.zeros_like(blk)))
    return jnp.concatenate(rows, axis=0)


def _dsa(main, vt, iwt, bsz):
    t = main.shape[0]
    n_qt = SEQ // TQ_DSA
    return pl.pallas_call(
        _dsa_kernel,
        grid=(bsz, n_qt),
        in_specs=[
            pl.BlockSpec((SEQ, LANES), lambda b, i: (b, _MAIN_OFF["i_kk"] // LANES)),
            pl.BlockSpec((SEQ, LANES), lambda b, i: (b, _MAIN_OFF["b_kk"] // LANES)),
            pl.BlockSpec((LANES, SEQ), lambda b, i: (A_WIDTH // LANES, b)),
            pl.BlockSpec((TQ_DSA, 2 * LANES), lambda b, i: (b * n_qt + i, _MAIN_OFF["i_q"] // (2 * LANES))),
            pl.BlockSpec((IWT_ROWS, TQ_DSA), lambda b, i: (0, b * n_qt + i)),
            pl.BlockSpec((TQ_DSA, B_WIDTH), lambda b, i: (b * n_qt + i, _MAIN_OFF["b_q"] // B_WIDTH)),
        ],
        out_specs=pl.BlockSpec((TQ_DSA, B_WIDTH), lambda b, i: (b * n_qt + i, 0)),
        out_shape=jax.ShapeDtypeStruct((t, B_WIDTH), jnp.bfloat16),
        scratch_shapes=[
            pltpu.VMEM((SEQ, TQ_DSA), jnp.int32),
            pltpu.VMEM((SEQ, TQ_DSA), jnp.int16),
            pltpu.VMEM((SEQ, TQ_DSA), jnp.float32),
        ],
        compiler_params=pltpu.CompilerParams(
            dimension_semantics=("parallel", "arbitrary"), vmem_limit_bytes=VMEM_LIMIT),
        name="dsa",
    )(main, main, vt, main, iwt, main)


N_CBLK = SEQ // C_BLOCK
N_GROUPS = len(C_GROUPS)


def _dilated_kernel(*refs):
    q_in = refs[0:N_GROUPS]
    k_in = refs[N_GROUPS:2 * N_GROUPS]
    v_in = refs[2 * N_GROUPS:3 * N_GROUPS]
    o_ref = refs[3 * N_GROUPS]
    scratch = refs[3 * N_GROUPS + 1:]
    tmp_ref = scratch[0]
    dense = scratch[1:1 + 3 * (N_GROUPS - 1)]
    od_ref, ld_ref = scratch[-2 - 2 * N_GROUPS], scratch[-1 - 2 * N_GROUPS]
    o_tok = scratch[-2 * N_GROUPS:-N_GROUPS]
    l_tok = scratch[-N_GROUPS:]

    lane = lax.broadcasted_iota(jnp.int32, (1, LANES), 1)
    low = lane < HEAD_DIM
    qi = lax.broadcasted_iota(jnp.int32, (C_BLOCK, C_BLOCK), 0)
    ki = lax.broadcasted_iota(jnp.int32, (C_BLOCK, C_BLOCK), 1)
    own_ok = ki <= qi

    for g, (window, dil) in enumerate(C_GROUPS):
        assert window // dil == C_BLOCK
        n_sub = SEQ // dil
        n_blk = n_sub // C_BLOCK
        if dil == 1:
            qd, kd, vd = q_in[g], k_in[g], v_in[g]
        else:
            qd, kd, vd = dense[3 * (g - 1):3 * g]
            for src, dst in ((q_in[g], qd), (k_in[g], kd), (v_in[g], vd)):
                tmp_ref[...] = src[...].astype(jnp.float32)
                for r in range(dil):
                    dst[r * n_sub:(r + 1) * n_sub, :] = tmp_ref[pl.ds(r, n_sub, stride=dil), :].astype(dst.dtype)
        o_dst = o_tok[g] if dil == 1 else od_ref
        l_dst = l_tok[g] if dil == 1 else ld_ref

        blocked = (N_CBLK, C_BLOCK, LANES)
        q3 = qd[...].reshape(blocked)
        k3 = kd[...].reshape(blocked)
        v3 = vd[...].reshape(blocked)
        if n_blk > 1:
            pad = jnp.zeros((1, C_BLOCK, LANES), k3.dtype)
            k3 = jnp.concatenate([jnp.concatenate([pad, k3[:-1]], axis=0), k3], axis=1)
            v3 = jnp.concatenate([jnp.concatenate([pad, v3[:-1]], axis=0), v3], axis=1)
            shape = (N_CBLK, C_BLOCK, 2 * C_BLOCK)
            b_i = lax.broadcasted_iota(jnp.int32, shape, 0)
            q_i = lax.broadcasted_iota(jnp.int32, shape, 1)
            k_i = lax.broadcasted_iota(jnp.int32, shape, 2)
            lo = jnp.minimum(q_i + jnp.where((b_i & (n_blk - 1)) == 0, C_BLOCK, 0), C_BLOCK)
            bias = jnp.where(k_i >= lo, jnp.where(k_i <= q_i + C_BLOCK, 0.0, NEG_INF), NEG_INF)
        else:
            bias = jnp.where(own_ok, 0.0, NEG_INF)[None]
        o_h, l_h = [], []
        for hh in range(2):
            qh = jnp.where(_head_mask(hh)[None], q3, jnp.zeros_like(q3))
            s = jnp.einsum("bqd,bkd->bqk", qh, k3, preferred_element_type=jnp.float32) + bias
            m = jnp.max(s, axis=-1, keepdims=True)
            p = jnp.exp2(s - m)
            l = jnp.sum(p, axis=-1, keepdims=True)
            o = jnp.einsum("bqk,bkd->bqd", p.astype(jnp.bfloat16), v3, preferred_element_type=jnp.float32)
            o_h.append(o / l)
            l_h.append(m + jnp.log2(l))
        o_dst[...] = jnp.where(low[None], o_h[0], o_h[1]).reshape(SEQ, LANES)
        l_dst[...] = jnp.where(low[None], l_h[0], l_h[1]).reshape(SEQ, LANES)
        if dil > 1:
            for r in range(dil):
                o_tok[g][pl.ds(r, n_sub, stride=dil), :] = od_ref[r * n_sub:(r + 1) * n_sub, :]
                l_tok[g][pl.ds(r, n_sub, stride=dil), :] = ld_ref[r * n_sub:(r + 1) * n_sub, :]

    lses = [l_tok[g][...] for g in range(N_GROUPS)]
    top = functools.reduce(jnp.maximum, lses)
    es = [jnp.exp2(x - top) for x in lses]
    num = sum(e * o_tok[g][...] for g, e in enumerate(es))
    o_ref[...] = (num / sum(es)).astype(o_ref.dtype)


def _dilated(main, bsz):
    t = main.shape[0]
    n_pairs = C_SLOTS // 2

    def col_spec(name, g):
        base = _MAIN_OFF[name] // LANES + g * n_pairs
        return pl.BlockSpec((SEQ, LANES), lambda b, p: (b, base + p))

    in_specs = ([col_spec("c_q", g) for g in range(N_GROUPS)]
                + [col_spec("c_k", g) for g in range(N_GROUPS)]
                + [col_spec("c_v", g) for g in range(N_GROUPS)])
    scratch = ([pltpu.VMEM((SEQ, LANES), jnp.float32)]
               + [pltpu.VMEM((SEQ, LANES), jnp.bfloat16)] * (3 * (N_GROUPS - 1))
               + [pltpu.VMEM((SEQ, LANES), jnp.float32)] * (2 + 2 * N_GROUPS))
    return pl.pallas_call(
        _dilated_kernel,
        grid=(bsz, n_pairs),
        in_specs=in_specs,
        out_specs=pl.BlockSpec((SEQ, LANES), lambda b, p: (b, p)),
        out_shape=jax.ShapeDtypeStruct((t, C_WIDTH), jnp.bfloat16),
        scratch_shapes=scratch,
        compiler_params=pltpu.CompilerParams(
            dimension_semantics=("parallel", "parallel"), vmem_limit_bytes=VMEM_LIMIT),
        name="dilated",
    )(*([main] * (3 * N_GROUPS)))


def _sigmoid(x):
    return 1.0 / (1.0 + jnp.exp(-x))


def _merge_kernel(x_ref, g_ref, ya_ref, yb_ref, yc_ref, wg_ref, wa_ref, wb_ref, wc_ref, wo_ref, fg_ref,
                  o_ref, *, final):
    x = x_ref[...]
    h = _rms(x, g_ref[...]).astype(jnp.bfloat16)
    merged = jnp.zeros(x.shape, jnp.float32)
    c0 = 0
    m0 = A_WIDTH + B_WIDTH + C_WIDTH
    for i, (y_ref, w_ref) in enumerate(((ya_ref, wa_ref), (yb_ref, wb_ref), (yc_ref, wc_ref))):
        width = y_ref.shape[1]
        gate = _dot(h, wg_ref[:, c0:c0 + width])
        z = (y_ref[...].astype(jnp.float32) * (gate * _sigmoid(gate))).astype(jnp.bfloat16)
        mix = _sigmoid(_dot(h, wg_ref[:, m0 + i * D_MODEL:m0 + (i + 1) * D_MODEL]))
        merged = merged + mix * _dot(z, w_ref[...])
        c0 += width
    out = x + _dot(merged.astype(jnp.bfloat16), wo_ref[...])
    if final:
        out = _rms(out, fg_ref[...])
    o_ref[...] = out


def _merge(x2, g, ya, yb, yc, w_gate, w_a, w_b, w_c, w_o, fg, final):
    t = x2.shape[0]
    row = lambda w: pl.BlockSpec((TM_MERGE, w), lambda i: (i, 0))
    full = lambda a: pl.BlockSpec(a.shape, lambda i: (0, 0))
    return pl.pallas_call(
        functools.partial(_merge_kernel, final=final),
        grid=(t // TM_MERGE,),
        in_specs=[row(D_MODEL), full(g), row(A_WIDTH), row(B_WIDTH), row(C_WIDTH),
                  full(w_gate), full(w_a), full(w_b), full(w_c), full(w_o), full(fg)],
        out_specs=row(D_MODEL),
        out_shape=jax.ShapeDtypeStruct((t, D_MODEL), jnp.float32),
        compiler_params=pltpu.CompilerParams(dimension_semantics=("parallel",),
                                             vmem_limit_bytes=VMEM_LIMIT),
        name="merge_final" if final else "merge",
    )(x2, g, ya, yb, yc, w_gate, w_a, w_b, w_c, w_o, fg)


def _rope_tables():
    inv_freq = 1.0 / (ROPE_THETA ** (jnp.arange(0, HEAD_DIM, 2, dtype=jnp.float32) / HEAD_DIM))
    ang = jnp.arange(SEQ, dtype=jnp.float32)[:, None] * inv_freq[None, :]
    cos, sin = jnp.cos(ang), jnp.sin(ang)
    reps = LANES // HEAD_DIM
    cos_t = jnp.tile(jnp.concatenate([cos, cos], axis=1), (1, reps))
    sin_t = jnp.tile(jnp.concatenate([-sin, sin], axis=1), (1, reps))
    return cos_t, sin_t


def _split_w_in(w):
    names = ("a_q", "a_k", "a_v", "a_g", "b_q", "b_k", "b_v", "b_g", "i_q", "i_k", "i_w",
             "c_q", "c_k", "c_v", "c_g", "m_g")
    return {n: w[:, _OFF[i]:_OFF[i + 1]] for i, n in enumerate(names)}


def _layer_weights(w_in):
    p = _split_w_in(w_in)
    scale = HEAD_DIM ** -0.5 * LOG2_E
    main = jnp.concatenate([
        p["a_q"] * scale, p["a_k"], p["b_q"] * scale, p["b_k"], p["b_k"],
        p["i_q"], p["i_k"], p["i_k"], p["c_q"] * scale, p["c_k"], p["c_v"]], axis=1)
    iw_scale = (B_IDX_HEADS * B_IDX_DIM) ** -0.5
    w_t = jnp.concatenate([
        p["a_v"].T, p["b_v"].T, p["b_v"].T, p["i_w"].T * iw_scale,
        jnp.zeros((IWT_ROWS - B_IDX_HEADS, D_MODEL), w_in.dtype)], axis=0)
    gates = jnp.concatenate([p["a_g"], p["b_g"], p["c_g"], p["m_g"]], axis=1)
    bf = jnp.bfloat16
    return main.astype(bf), w_t.astype(bf), gates.astype(bf)


def kernel(x, norm_g, w_in, w_br_a, w_br_b, w_br_c, w_out, final_norm_g):
    bsz, seq, d_model = x.shape
    assert seq == SEQ and d_model == D_MODEL
    assert MAIN_WIDTH % LANES == 0 and GATE_WIDTH == sum(IN_SPLITS[i] for i in (3, 7, 14, 15))
    cos_t, sin_t = _rope_tables()
    bf = jnp.bfloat16
    x2 = x.reshape(bsz * seq, d_model)
    fg = final_norm_g.reshape(1, d_model)
    for layer in range(DEPTH):
        w_main, w_t, w_gate = _layer_weights(w_in[layer])
        g = norm_g[layer].reshape(1, d_model)
        main, vt, iwt = _proj(x2, g, w_main, w_t, cos_t, sin_t)
        ya = _moba(main, vt, bsz)
        yb = _dsa(main, vt, iwt, bsz)
        yc = _dilated(main, bsz)
        x2 = _merge(x2, g, ya, yb, yc, w_gate, w_br_a[layer].astype(bf), w_br_b[layer].astype(bf),
                    w_br_c[layer].astype(bf), w_out[layer].astype(bf), fg, final=(layer == DEPTH - 1))
    return x2.reshape(bsz, seq, d_model)
```

```python
import functools

import numpy as np
import jax
import jax.numpy as jnp
from jax import lax
from jax.experimental import pallas as pl
from jax.experimental.pallas import tpu as pltpu

D_MODEL = 1024
SEQ = 2048
DEPTH = 2
HEAD_DIM = 64
ROPE_THETA = 10000.0
RMS_EPS = 1e-6
N_BRANCH = 3

A_HEADS = 6
A_WIDTH = A_HEADS * HEAD_DIM
A_BLOCK = 256
A_TOPK = 3

B_HEADS = 6
B_WIDTH = B_HEADS * HEAD_DIM
B_TOPK = 256
B_IDX_HEADS = 4
B_IDX_DIM = HEAD_DIM

C_GROUPS = ((128, 1), (512, 4), (2048, 16))
C_SLOTS = 4
C_HEADS = C_SLOTS * len(C_GROUPS)
C_WIDTH = C_SLOTS * HEAD_DIM
C_BLOCK = 128

IN_SPLITS = (A_WIDTH, A_WIDTH, A_WIDTH, A_WIDTH,
             B_WIDTH, HEAD_DIM, HEAD_DIM, B_WIDTH, B_IDX_HEADS * B_IDX_DIM, B_IDX_DIM, B_IDX_HEADS,
             C_HEADS * HEAD_DIM, C_HEADS * HEAD_DIM, C_HEADS * HEAD_DIM, C_WIDTH,
             N_BRANCH * D_MODEL)
_OFF = tuple(int(o) for o in np.cumsum((0,) + IN_SPLITS))

LANES = 128
VMEM_LIMIT = 56 * 1024 * 1024

_MAIN_GROUPS = (("a_q", A_WIDTH), ("a_k", A_WIDTH), ("b_q", B_WIDTH), ("b_kk", LANES),
                ("i_q", B_IDX_HEADS * B_IDX_DIM), ("i_kk", LANES),
                ("c_q", C_HEADS * HEAD_DIM), ("c_k", C_HEADS * HEAD_DIM), ("c_v", C_HEADS * HEAD_DIM))
_MAIN_OFF = {}
_c = 0
for _n, _w in _MAIN_GROUPS:
    _MAIN_OFF[_n] = _c
    _c += _w
MAIN_WIDTH = _c
ROPE_WIDTH = _MAIN_OFF["c_v"]
VT_ROWS = A_WIDTH + LANES
IWT_ROWS = 8
GATE_WIDTH = A_WIDTH + B_WIDTH + C_WIDTH + N_BRANCH * D_MODEL

TM_PROJ = 512
TM_MERGE = 512
TQ_DSA = 128
INT_MIN = -2 ** 31
LOG2_E = 1.4426950408889634
NEG_INF = float("-inf")


def _dot(a, b):
    return jnp.dot(a, b, preferred_element_type=jnp.float32)


def _dot_nt(a, b):
    return lax.dot_general(a, b, (((1,), (1,)), ((), ())), preferred_element_type=jnp.float32)


def _colsum(x):
    n = x.shape[0]
    part = x.reshape(n // LANES, LANES, x.shape[1]).sum(axis=0)
    return part.sum(axis=0, keepdims=True)


def _colmax(x):
    n = x.shape[0]
    part = x.reshape(n // LANES, LANES, x.shape[1]).max(axis=0)
    return part.max(axis=0, keepdims=True)


def _rms(x, g):
    return x * lax.rsqrt(jnp.mean(x * x, axis=-1, keepdims=True) + RMS_EPS) * g


def _head_mask(hh):
    lane = lax.broadcasted_iota(jnp.int32, (1, LANES), 1)
    return (lane >= hh * HEAD_DIM) & (lane < (hh + 1) * HEAD_DIM)


def _proj_kernel(x_ref, g_ref, w_ref, wt_ref, cos_ref, sin_ref, main_ref, vt_ref, iwt_ref):
    h = _rms(x_ref[...], g_ref[...]).astype(jnp.bfloat16)
    cos = cos_ref[...]
    sin = sin_ref[...]
    lane = lax.broadcasted_iota(jnp.int32, (1, LANES), 1)
    first_half = (lane % HEAD_DIM) < (HEAD_DIM // 2)
    chunk = 4 * LANES
    for c0 in range(0, MAIN_WIDTH, chunk):
        w = min(chunk, MAIN_WIDTH - c0)
        res = _dot(h, w_ref[:, c0:c0 + w])
        for j in range(0, w, LANES):
            blk = res[:, j:j + LANES]
            if c0 + j < ROPE_WIDTH:
                partner = jnp.where(first_half, pltpu.roll(blk, LANES - HEAD_DIM // 2, 1),
                                    pltpu.roll(blk, HEAD_DIM // 2, 1))
                blk = blk * cos + partner * sin
            main_ref[:, c0 + j:c0 + j + LANES] = blk.astype(main_ref.dtype)
    rt = _dot_nt(wt_ref[...], h)
    vt_ref[...] = rt[:VT_ROWS].astype(vt_ref.dtype)
    iwt_ref[...] = rt[VT_ROWS:]


def _proj(x2, g, w_main, w_t, cos_t, sin_t):
    t = x2.shape[0]
    n_rope_tiles = SEQ // TM_PROJ
    return pl.pallas_call(
        _proj_kernel,
        grid=(t // TM_PROJ,),
        in_specs=[
            pl.BlockSpec((TM_PROJ, D_MODEL), lambda i: (i, 0)),
            pl.BlockSpec((1, D_MODEL), lambda i: (0, 0)),
            pl.BlockSpec((D_MODEL, MAIN_WIDTH), lambda i: (0, 0)),
            pl.BlockSpec((VT_ROWS + IWT_ROWS, D_MODEL), lambda i: (0, 0)),
            pl.BlockSpec((TM_PROJ, LANES), lambda i: (i % n_rope_tiles, 0)),
            pl.BlockSpec((TM_PROJ, LANES), lambda i: (i % n_rope_tiles, 0)),
        ],
        out_specs=[
            pl.BlockSpec((TM_PROJ, MAIN_WIDTH), lambda i: (i, 0)),
            pl.BlockSpec((VT_ROWS, TM_PROJ), lambda i: (0, i)),
            pl.BlockSpec((IWT_ROWS, TM_PROJ), lambda i: (0, i)),
        ],
        out_shape=[
            jax.ShapeDtypeStruct((t, MAIN_WIDTH), jnp.bfloat16),
            jax.ShapeDtypeStruct((VT_ROWS, t), jnp.bfloat16),
            jax.ShapeDtypeStruct((IWT_ROWS, t), jnp.float32),
        ],
        compiler_params=pltpu.CompilerParams(dimension_semantics=("parallel",),
                                             vmem_limit_bytes=VMEM_LIMIT),
        name="proj",
    )(x2, g, w_main, w_t, cos_t, sin_t)


N_ABLK = SEQ // A_BLOCK


def _moba_kernel(q_ref, k_ref, vt_ref, o_ref, kmean_ref):
    qt = pl.program_id(2)

    @pl.when(qt == 0)
    def _():
        for n in range(N_ABLK):
            kb = k_ref[n * A_BLOCK:(n + 1) * A_BLOCK, :].astype(jnp.float32)
            kmean_ref[n:n + 1, :] = jnp.mean(kb, axis=0, keepdims=True)

    for own in range(N_ABLK):
        @pl.when(qt == own)
        def _(own=own):
            _moba_body(own, q_ref, k_ref, vt_ref, o_ref, kmean_ref)


def _moba_body(own, q_ref, k_ref, vt_ref, o_ref, kmean_ref):
    nk = (own + 1) * A_BLOCK
    q = q_ref[...]
    kmean = kmean_ref[...].astype(jnp.bfloat16)
    cols = 2 * A_BLOCK
    blk_f = lax.broadcasted_iota(jnp.int32, (N_ABLK, cols), 0).astype(jnp.float32)
    kpos = lax.broadcasted_iota(jnp.int32, (A_BLOCK, cols), 0)
    qpos = lax.broadcasted_iota(jnp.int32, (A_BLOCK, cols), 1) & (A_BLOCK - 1)
    causal = kpos <= qpos
    q2 = jnp.concatenate([jnp.where(_head_mask(hh), q, jnp.zeros_like(q)) for hh in range(2)], axis=0)
    gate = _dot_nt(kmean, q2)
    gate = jnp.where(blk_f < float(own), gate, NEG_INF)
    sel = jnp.zeros(gate.shape, jnp.float32)
    for _ in range(min(A_TOPK, own)):
        top = jnp.max(gate, axis=0, keepdims=True)
        is_top = (gate == top) & (top > NEG_INF)
        idx = jnp.min(jnp.where(is_top, blk_f, float(N_ABLK)), axis=0, keepdims=True)
        pick = blk_f == idx
        sel = jnp.where(pick, 1.0, sel)
        gate = jnp.where(pick, NEG_INF, gate)
    s = _dot_nt(k_ref[0:nk, :], q2)
    parts = [jnp.where(sel[n:n + 1, :] > 0.0, s[n * A_BLOCK:(n + 1) * A_BLOCK], NEG_INF)
             for n in range(own)]
    parts.append(jnp.where(causal, s[own * A_BLOCK:], NEG_INF))
    s = jnp.concatenate(parts, axis=0)
    m = _colmax(s)
    p = jnp.exp2(s - m)
    l = _colsum(p)
    o = _dot(vt_ref[:, 0:nk], p.astype(jnp.bfloat16)) / l
    o = jnp.concatenate([o[hh * HEAD_DIM:(hh + 1) * HEAD_DIM, hh * A_BLOCK:(hh + 1) * A_BLOCK]
                         for hh in range(2)], axis=0)
    o_ref[...] = o.T.astype(o_ref.dtype)


def _moba(main, vt, bsz):
    t = main.shape[0]
    n_pairs = A_HEADS // 2
    q_col = _MAIN_OFF["a_q"] // LANES
    k_col = _MAIN_OFF["a_k"] // LANES
    return pl.pallas_call(
        _moba_kernel,
        grid=(bsz, n_pairs, N_ABLK),
        in_specs=[
            pl.BlockSpec((A_BLOCK, LANES), lambda b, p, i: (b * N_ABLK + i, q_col + p)),
            pl.BlockSpec((SEQ, LANES), lambda b, p, i: (b, k_col + p)),
            pl.BlockSpec((LANES, SEQ), lambda b, p, i: (p, b)),
        ],
        out_specs=pl.BlockSpec((A_BLOCK, LANES), lambda b, p, i: (b * N_ABLK + i, p)),
        out_shape=jax.ShapeDtypeStruct((t, A_WIDTH), jnp.bfloat16),
        scratch_shapes=[pltpu.VMEM((N_ABLK, LANES), jnp.float32)],
        compiler_params=pltpu.CompilerParams(
            dimension_semantics=("parallel", "parallel", "arbitrary"), vmem_limit_bytes=VMEM_LIMIT),
        name="moba",
    )(main, main, vt)


TIE_CHUNK = 256


KV_STEP = 256


def _dsa_kernel(ikk_ref, kk_ref, vvt_ref, iq_ref, iwt_ref, q_ref, o_ref, u_ref, bias_ref):
    qt = pl.program_id(1)
    tiles_per_step = KV_STEP // TQ_DSA
    for c in range(SEQ // KV_STEP):
        @pl.when(qt // tiles_per_step == c)
        def _(c=c):
            _dsa_body((c + 1) * KV_STEP, qt, ikk_ref, kk_ref, vvt_ref, iq_ref, iwt_ref, q_ref, o_ref,
                      u_ref, bias_ref)


def _dsa_body(nk, qt, ikk_ref, kk_ref, vvt_ref, iq_ref, iwt_ref, q_ref, o_ref, u_ref, bias_ref):
    n_keep = float(B_TOPK)
    key_pos = lax.broadcasted_iota(jnp.int32, (nk, TQ_DSA), 0)
    q_pos = qt * TQ_DSA + lax.broadcasted_iota(jnp.int32, (nk, TQ_DSA), 1)
    causal = key_pos <= q_pos

    iw = iwt_ref[...]
    iq_all = _stack_heads(iq_ref[...], B_IDX_HEADS)
    iw_all = jnp.concatenate([iw[h:h + 1, :] for h in range(B_IDX_HEADS)], axis=1)
    part = jnp.maximum(_dot_nt(ikk_ref[0:nk, :], iq_all), 0.0) * iw_all
    score = part[:, 0:TQ_DSA]
    for h in range(1, B_IDX_HEADS):
        score = score + part[:, h * TQ_DSA:(h + 1) * TQ_DSA]

    score = jnp.where(score == 0.0, 0.0, score)
    bits = pltpu.bitcast(score, jnp.int32)
    u = jnp.where(bits < 0, bits ^ jnp.int32(0x7FFFFFFF), bits)
    u_ref[0:nk, :] = jnp.where(causal, u, jnp.int32(INT_MIN))

    def bisect(i, tau):
        trial = tau ^ jnp.left_shift(jnp.int32(1), 31 - i)
        acc = jnp.zeros((LANES, TQ_DSA), jnp.float32)
        for c in range(nk // LANES):
            acc = jnp.where(u_ref[c * LANES:(c + 1) * LANES, :] >= trial, acc + 1.0, acc)
        cnt = acc.sum(axis=0, keepdims=True)
        return jnp.where(cnt >= n_keep, trial, tau)

    tau = lax.fori_loop(0, 32, bisect, jnp.full((1, TQ_DSA), INT_MIN, jnp.int32))
    bounded = tau > jnp.int32(INT_MIN)

    ge = u_ref[0:nk, :] >= jnp.maximum(tau, jnp.int32(INT_MIN + 1))
    n_ge = _colsum(jnp.where(ge, 1.0, 0.0))
    bias_ref[0:nk, :] = jnp.where(ge, 0.0, NEG_INF)
    excess = jnp.where(bounded, n_ge - n_keep, 0.0)

    @pl.when(jnp.max(excess) > 0.0)
    def _():
        n_gt = _colsum(jnp.where(u_ref[0:nk, :] > tau, 1.0, 0.0))
        need = n_keep - n_gt
        r_i = lax.broadcasted_iota(jnp.int32, (TIE_CHUNK, TIE_CHUNK), 0)
        c_i = lax.broadcasted_iota(jnp.int32, (TIE_CHUNK, TIE_CHUNK), 1)
        below = jnp.where(c_i < r_i, 1.0, 0.0).astype(jnp.bfloat16)
        seen = jnp.zeros((1, TQ_DSA), jnp.float32)
        for c in range(nk // TIE_CHUNK):
            uc = u_ref[c * TIE_CHUNK:(c + 1) * TIE_CHUNK, :]
            eq = jnp.where((uc == tau) & bounded, 1.0, 0.0)
            rank = _dot(below, eq.astype(jnp.bfloat16)) + seen
            seen = seen + jnp.sum(eq, axis=0, keepdims=True)
            keep = (uc > tau) | ((eq > 0.0) & (rank < need))
            bias_ref[c * TIE_CHUNK:(c + 1) * TIE_CHUNK, :] = jnp.where(keep, 0.0, NEG_INF)

    q_all = _stack_heads(q_ref[...], B_HEADS)
    bias = bias_ref[0:nk, :]
    s = _dot_nt(kk_ref[0:nk, :], q_all) + jnp.concatenate([bias] * B_HEADS, axis=1)
    m = _colmax(s)
    p = jnp.exp2(s - m)
    l = _colsum(p)
    o = _dot(vvt_ref[:, 0:nk], p.astype(jnp.bfloat16)) / l
    o = jnp.concatenate([o[(h % 2) * HEAD_DIM:(h % 2 + 1) * HEAD_DIM, h * TQ_DSA:(h + 1) * TQ_DSA]
                         for h in range(B_HEADS)], axis=0)
    o_ref[...] = o.T.astype(o_ref.dtype)


def _stack_heads(x, n_heads):
    rows = []
    for h in range(n_heads):
        blk = x[:, (h // 2) * LANES:(h // 2 + 1) * LANES]
        rows.append(jnp.where(_head_mask(h % 2), blk, jnp.zeros_like(blk)))
    return jnp.concatenate(rows, axis=0)


def _dsa(main, vt, iwt, bsz):
    t = main.shape[0]
    n_qt = SEQ // TQ_DSA
    return pl.pallas_call(
        _dsa_kernel,
        grid=(bsz, n_qt),
        in_specs=[
            pl.BlockSpec((SEQ, LANES), lambda b, i: (b, _MAIN_OFF["i_kk"] // LANES)),
            pl.BlockSpec((SEQ, LANES), lambda b, i: (b, _MAIN_OFF["b_kk"] // LANES)),
            pl.BlockSpec((LANES, SEQ), lambda b, i: (A_WIDTH // LANES, b)),
            pl.BlockSpec((TQ_DSA, 2 * LANES), lambda b, i: (b * n_qt + i, _MAIN_OFF["i_q"] // (2 * LANES))),
            pl.BlockSpec((IWT_ROWS, TQ_DSA), lambda b, i: (0, b * n_qt + i)),
            pl.BlockSpec((TQ_DSA, B_WIDTH), lambda b, i: (b * n_qt + i, _MAIN_OFF["b_q"] // B_WIDTH)),
        ],
        out_specs=pl.BlockSpec((TQ_DSA, B_WIDTH), lambda b, i: (b * n_qt + i, 0)),
        out_shape=jax.ShapeDtypeStruct((t, B_WIDTH), jnp.bfloat16),
        scratch_shapes=[
            pltpu.VMEM((SEQ, TQ_DSA), jnp.int32),
            pltpu.VMEM((SEQ, TQ_DSA), jnp.float32),
        ],
        compiler_params=pltpu.CompilerParams(
            dimension_semantics=("parallel", "arbitrary"), vmem_limit_bytes=VMEM_LIMIT),
        name="dsa",
    )(main, main, vt, main, iwt, main)


N_CBLK = SEQ // C_BLOCK
N_GROUPS = len(C_GROUPS)


def _dilated_kernel(*refs):
    q_in = refs[0:N_GROUPS]
    k_in = refs[N_GROUPS:2 * N_GROUPS]
    v_in = refs[2 * N_GROUPS:3 * N_GROUPS]
    o_ref = refs[3 * N_GROUPS]
    scratch = refs[3 * N_GROUPS + 1:]
    tmp_ref = scratch[0]
    dense = scratch[1:1 + 3 * (N_GROUPS - 1)]
    od_ref, ld_ref = scratch[-2 - 2 * N_GROUPS], scratch[-1 - 2 * N_GROUPS]
    o_tok = scratch[-2 * N_GROUPS:-N_GROUPS]
    l_tok = scratch[-N_GROUPS:]

    lane = lax.broadcasted_iota(jnp.int32, (1, LANES), 1)
    low = lane < HEAD_DIM
    qi = lax.broadcasted_iota(jnp.int32, (C_BLOCK, C_BLOCK), 0)
    ki = lax.broadcasted_iota(jnp.int32, (C_BLOCK, C_BLOCK), 1)
    own_ok = ki <= qi

    for g, (window, dil) in enumerate(C_GROUPS):
        assert window // dil == C_BLOCK
        n_sub = SEQ // dil
        n_blk = n_sub // C_BLOCK
        if dil == 1:
            qd, kd, vd = q_in[g], k_in[g], v_in[g]
        else:
            qd, kd, vd = dense[3 * (g - 1):3 * g]
            for src, dst in ((q_in[g], qd), (k_in[g], kd), (v_in[g], vd)):
                tmp_ref[...] = src[...].astype(jnp.float32)
                for r in range(dil):
                    dst[r * n_sub:(r + 1) * n_sub, :] = tmp_ref[pl.ds(r, n_sub, stride=dil), :].astype(dst.dtype)
        o_dst = o_tok[g] if dil == 1 else od_ref
        l_dst = l_tok[g] if dil == 1 else ld_ref

        blocked = (N_CBLK, C_BLOCK, LANES)
        q3 = qd[...].reshape(blocked)
        k3 = kd[...].reshape(blocked)
        v3 = vd[...].reshape(blocked)
        if n_blk > 1:
            pad = jnp.zeros((1, C_BLOCK, LANES), k3.dtype)
            k3 = jnp.concatenate([jnp.concatenate([pad, k3[:-1]], axis=0), k3], axis=1)
            v3 = jnp.concatenate([jnp.concatenate([pad, v3[:-1]], axis=0), v3], axis=1)
            shape = (N_CBLK, C_BLOCK, 2 * C_BLOCK)
            b_i = lax.broadcasted_iota(jnp.int32, shape, 0)
            q_i = lax.broadcasted_iota(jnp.int32, shape, 1)
            k_i = lax.broadcasted_iota(jnp.int32, shape, 2)
            lo = jnp.minimum(q_i + jnp.where((b_i & (n_blk - 1)) == 0, C_BLOCK, 0), C_BLOCK)
            bias = jnp.where(k_i >= lo, jnp.where(k_i <= q_i + C_BLOCK, 0.0, NEG_INF), NEG_INF)
        else:
            bias = jnp.where(own_ok, 0.0, NEG_INF)[None]
        o_h, l_h = [], []
        for hh in range(2):
            qh = jnp.where(_head_mask(hh)[None], q3, jnp.zeros_like(q3))
            s = jnp.einsum("bqd,bkd->bqk", qh, k3, preferred_element_type=jnp.float32) + bias
            m = jnp.max(s, axis=-1, keepdims=True)
            p = jnp.exp2(s - m)
            l = jnp.sum(p, axis=-1, keepdims=True)
            o = jnp.einsum("bqk,bkd->bqd", p.astype(jnp.bfloat16), v3, preferred_element_type=jnp.float32)
            o_h.append(o / l)
            l_h.append(m + jnp.log2(l))
        o_dst[...] = jnp.where(low[None], o_h[0], o_h[1]).reshape(SEQ, LANES)
        l_dst[...] = jnp.where(low[None], l_h[0], l_h[1]).reshape(SEQ, LANES)
        if dil > 1:
            for r in range(dil):
                o_tok[g][pl.ds(r, n_sub, stride=dil), :] = od_ref[r * n_sub:(r + 1) * n_sub, :]
                l_tok[g][pl.ds(r, n_sub, stride=dil), :] = ld_ref[r * n_sub:(r + 1) * n_sub, :]

    lses = [l_tok[g][...] for g in range(N_GROUPS)]
    top = functools.reduce(jnp.maximum, lses)
    es = [jnp.exp2(x - top) for x in lses]
    num = sum(e * o_tok[g][...] for g, e in enumerate(es))
    o_ref[...] = (num / sum(es)).astype(o_ref.dtype)


def _dilated(main, bsz):
    t = main.shape[0]
    n_pairs = C_SLOTS // 2

    def col_spec(name, g):
        base = _MAIN_OFF[name] // LANES + g * n_pairs
        return pl.BlockSpec((SEQ, LANES), lambda b, p: (b, base + p))

    in_specs = ([col_spec("c_q", g) for g in range(N_GROUPS)]
                + [col_spec("c_k", g) for g in range(N_GROUPS)]
                + [col_spec("c_v", g) for g in range(N_GROUPS)])
    scratch = ([pltpu.VMEM((SEQ, LANES), jnp.float32)]
               + [pltpu.VMEM((SEQ, LANES), jnp.bfloat16)] * (3 * (N_GROUPS - 1))
               + [pltpu.VMEM((SEQ, LANES), jnp.float32)] * (2 + 2 * N_GROUPS))
    return pl.pallas_call(
        _dilated_kernel,
        grid=(bsz, n_pairs),
        in_specs=in_specs,
        out_specs=pl.BlockSpec((SEQ, LANES), lambda b, p: (b, p)),
        out_shape=jax.ShapeDtypeStruct((t, C_WIDTH), jnp.bfloat16),
        scratch_shapes=scratch,
        compiler_params=pltpu.CompilerParams(
            dimension_semantics=("parallel", "parallel"), vmem_limit_bytes=VMEM_LIMIT),
        name="dilated",
    )(*([main] * (3 * N_GROUPS)))


def _sigmoid(x):
    return 1.0 / (1.0 + jnp.exp(-x))


def _merge_kernel(x_ref, g_ref, ya_ref, yb_ref, yc_ref, wg_ref, wa_ref, wb_ref, wc_ref, wo_ref, fg_ref,
                  o_ref, *, final):
    x = x_ref[...]
    h = _rms(x, g_ref[...]).astype(jnp.bfloat16)
    merged = jnp.zeros(x.shape, jnp.float32)
    c0 = 0
    m0 = A_WIDTH + B_WIDTH + C_WIDTH
    for i, (y_ref, w_ref) in enumerate(((ya_ref, wa_ref), (yb_ref, wb_ref), (yc_ref, wc_ref))):
        width = y_ref.shape[1]
        gate = _dot(h, wg_ref[:, c0:c0 + width])
        z = (y_ref[...].astype(jnp.float32) * (gate * _sigmoid(gate))).astype(jnp.bfloat16)
        mix = _sigmoid(_dot(h, wg_ref[:, m0 + i * D_MODEL:m0 + (i + 1) * D_MODEL]))
        merged = merged + mix * _dot(z, w_ref[...])
        c0 += width
    out = x + _dot(merged.astype(jnp.bfloat16), wo_ref[...])
    if final:
        out = _rms(out, fg_ref[...])
    o_ref[...] = out


def _merge(x2, g, ya, yb, yc, w_gate, w_a, w_b, w_c, w_o, fg, final):
    t = x2.shape[0]
    row = lambda w: pl.BlockSpec((TM_MERGE, w), lambda i: (i, 0))
    full = lambda a: pl.BlockSpec(a.shape, lambda i: (0, 0))
    return pl.pallas_call(
        functools.partial(_merge_kernel, final=final),
        grid=(t // TM_MERGE,),
        in_specs=[row(D_MODEL), full(g), row(A_WIDTH), row(B_WIDTH), row(C_WIDTH),
                  full(w_gate), full(w_a), full(w_b), full(w_c), full(w_o), full(fg)],
        out_specs=row(D_MODEL),
        out_shape=jax.ShapeDtypeStruct((t, D_MODEL), jnp.float32),
        compiler_params=pltpu.CompilerParams(dimension_semantics=("parallel",),
                                             vmem_limit_bytes=VMEM_LIMIT),
        name="merge_final" if final else "merge",
    )(x2, g, ya, yb, yc, w_gate, w_a, w_b, w_c, w_o, fg)


def _rope_tables():
    inv_freq = 1.0 / (ROPE_THETA ** (jnp.arange(0, HEAD_DIM, 2, dtype=jnp.float32) / HEAD_DIM))
    ang = jnp.arange(SEQ, dtype=jnp.float32)[:, None] * inv_freq[None, :]
    cos, sin = jnp.cos(ang), jnp.sin(ang)
    reps = LANES // HEAD_DIM
    cos_t = jnp.tile(jnp.concatenate([cos, cos], axis=1), (1, reps))
    sin_t = jnp.tile(jnp.concatenate([-sin, sin], axis=1), (1, reps))
    return cos_t, sin_t


def _split_w_in(w):
    names = ("a_q", "a_k", "a_v", "a_g", "b_q", "b_k", "b_v", "b_g", "i_q", "i_k", "i_w",
             "c_q", "c_k", "c_v", "c_g", "m_g")
    return {n: w[:, _OFF[i]:_OFF[i + 1]] for i, n in enumerate(names)}


def _layer_weights(w_in):
    p = _split_w_in(w_in)
    scale = HEAD_DIM ** -0.5 * LOG2_E
    main = jnp.concatenate([
        p["a_q"] * scale, p["a_k"], p["b_q"] * scale, p["b_k"], p["b_k"],
        p["i_q"], p["i_k"], p["i_k"], p["c_q"] * scale, p["c_k"], p["c_v"]], axis=1)
    iw_scale = (B_IDX_HEADS * B_IDX_DIM) ** -0.5
    w_t = jnp.concatenate([
        p["a_v"].T, p["b_v"].T, p["b_v"].T, p["i_w"].T * iw_scale,
        jnp.zeros((IWT_ROWS - B_IDX_HEADS, D_MODEL), w_in.dtype)], axis=0)
    gates = jnp.concatenate([p["a_g"], p["b_g"], p["c_g"], p["m_g"]], axis=1)
    bf = jnp.bfloat16
    return main.astype(bf), w_t.astype(bf), gates.astype(bf)


def kernel(x, norm_g, w_in, w_br_a, w_br_b, w_br_c, w_out, final_norm_g):
    bsz, seq, d_model = x.shape
    assert seq == SEQ and d_model == D_MODEL
    assert MAIN_WIDTH % LANES == 0 and GATE_WIDTH == sum(IN_SPLITS[i] for i in (3, 7, 14, 15))
    cos_t, sin_t = _rope_tables()
    bf = jnp.bfloat16
    x2 = x.reshape(bsz * seq, d_model)
    fg = final_norm_g.reshape(1, d_model)
    for layer in range(DEPTH):
        w_main, w_t, w_gate = _layer_weights(w_in[layer])
        g = norm_g[layer].reshape(1, d_model)
        main, vt, iwt = _proj(x2, g, w_main, w_t, cos_t, sin_t)
        ya = _moba(main, vt, bsz)
        yb = _dsa(main, vt, iwt, bsz)
        yc = _dilated(main, bsz)
        x2 = _merge(x2, g, ya, yb, yc, w_gate, w_br_a[layer].astype(bf), w_br_b[layer].astype(bf),
                    w_br_c[layer].astype(bf), w_out[layer].astype(bf), fg, final=(layer == DEPTH - 1))
    return x2.reshape(bsz, seq, d_model)
```

```python
import functools

import numpy as np
import jax
import jax.numpy as jnp
from jax import lax
from jax.experimental import pallas as pl
from jax.experimental.pallas import tpu as pltpu

D_MODEL = 1024
SEQ = 2048
DEPTH = 2
HEAD_DIM = 64
ROPE_THETA = 10000.0
RMS_EPS = 1e-6
N_BRANCH = 3

A_HEADS = 6
A_WIDTH = A_HEADS * HEAD_DIM
A_BLOCK = 256
A_TOPK = 3

B_HEADS = 6
B_WIDTH = B_HEADS * HEAD_DIM
B_TOPK = 256
B_IDX_HEADS = 4
B_IDX_DIM = HEAD_DIM

C_GROUPS = ((128, 1), (512, 4), (2048, 16))
C_SLOTS = 4
C_HEADS = C_SLOTS * len(C_GROUPS)
C_WIDTH = C_SLOTS * HEAD_DIM
C_BLOCK = 128

IN_SPLITS = (A_WIDTH, A_WIDTH, A_WIDTH, A_WIDTH,
             B_WIDTH, HEAD_DIM, HEAD_DIM, B_WIDTH, B_IDX_HEADS * B_IDX_DIM, B_IDX_DIM, B_IDX_HEADS,
             C_HEADS * HEAD_DIM, C_HEADS * HEAD_DIM, C_HEADS * HEAD_DIM, C_WIDTH,
             N_BRANCH * D_MODEL)
_OFF = tuple(int(o) for o in np.cumsum((0,) + IN_SPLITS))

LANES = 128
VMEM_LIMIT = 56 * 1024 * 1024

_MAIN_GROUPS = (("a_q", A_WIDTH), ("a_k", A_WIDTH), ("b_q", B_WIDTH), ("b_kk", LANES),
                ("i_q", B_IDX_HEADS * B_IDX_DIM), ("i_kk", LANES),
                ("c_q", C_HEADS * HEAD_DIM), ("c_k", C_HEADS * HEAD_DIM), ("c_v", C_HEADS * HEAD_DIM))
_MAIN_OFF = {}
_c = 0
for _n, _w in _MAIN_GROUPS:
    _MAIN_OFF[_n] = _c
    _c += _w
MAIN_WIDTH = _c
ROPE_WIDTH = _MAIN_OFF["c_v"]
VT_ROWS = A_WIDTH + LANES
IWT_ROWS = 8
GATE_WIDTH = A_WIDTH + B_WIDTH + C_WIDTH + N_BRANCH * D_MODEL

TM_PROJ = 512
TM_MERGE = 512
TQ_DSA = 128
INT_MIN = -2 ** 31
LOG2_E = 1.4426950408889634
NEG_INF = float("-inf")


def _dot(a, b):
    return jnp.dot(a, b, preferred_element_type=jnp.float32)


def _dot_nt(a, b):
    return lax.dot_general(a, b, (((1,), (1,)), ((), ())), preferred_element_type=jnp.float32)


def _colsum(x):
    n = x.shape[0]
    part = x.reshape(n // LANES, LANES, x.shape[1]).sum(axis=0)
    return part.sum(axis=0, keepdims=True)


def _colmax(x):
    n = x.shape[0]
    part = x.reshape(n // LANES, LANES, x.shape[1]).max(axis=0)
    return part.max(axis=0, keepdims=True)


def _rms(x, g):
    return x * lax.rsqrt(jnp.mean(x * x, axis=-1, keepdims=True) + RMS_EPS) * g


def _head_mask(hh):
    lane = lax.broadcasted_iota(jnp.int32, (1, LANES), 1)
    return (lane >= hh * HEAD_DIM) & (lane < (hh + 1) * HEAD_DIM)


def _proj_kernel(x_ref, g_ref, w_ref, wt_ref, cos_ref, sin_ref, main_ref, vt_ref, iwt_ref):
    h = _rms(x_ref[...], g_ref[...]).astype(jnp.bfloat16)
    cos = cos_ref[...]
    sin = sin_ref[...]
    lane = lax.broadcasted_iota(jnp.int32, (1, LANES), 1)
    first_half = (lane % HEAD_DIM) < (HEAD_DIM // 2)
    chunk = 4 * LANES
    for c0 in range(0, MAIN_WIDTH, chunk):
        w = min(chunk, MAIN_WIDTH - c0)
        res = _dot(h, w_ref[:, c0:c0 + w])
        for j in range(0, w, LANES):
            blk = res[:, j:j + LANES]
            if c0 + j < ROPE_WIDTH:
                partner = jnp.where(first_half, pltpu.roll(blk, LANES - HEAD_DIM // 2, 1),
                                    pltpu.roll(blk, HEAD_DIM // 2, 1))
                blk = blk * cos + partner * sin
            main_ref[:, c0 + j:c0 + j + LANES] = blk.astype(main_ref.dtype)
    rt = _dot_nt(wt_ref[...], h)
    vt_ref[...] = rt[:VT_ROWS].astype(vt_ref.dtype)
    iwt_ref[...] = rt[VT_ROWS:]


def _proj(x2, g, w_main, w_t, cos_t, sin_t):
    t = x2.shape[0]
    n_rope_tiles = SEQ // TM_PROJ
    return pl.pallas_call(
        _proj_kernel,
        grid=(t // TM_PROJ,),
        in_specs=[
            pl.BlockSpec((TM_PROJ, D_MODEL), lambda i: (i, 0)),
            pl.BlockSpec((1, D_MODEL), lambda i: (0, 0)),
            pl.BlockSpec((D_MODEL, MAIN_WIDTH), lambda i: (0, 0)),
            pl.BlockSpec((VT_ROWS + IWT_ROWS, D_MODEL), lambda i: (0, 0)),
            pl.BlockSpec((TM_PROJ, LANES), lambda i: (i % n_rope_tiles, 0)),
            pl.BlockSpec((TM_PROJ, LANES), lambda i: (i % n_rope_tiles, 0)),
        ],
        out_specs=[
            pl.BlockSpec((TM_PROJ, MAIN_WIDTH), lambda i: (i, 0)),
            pl.BlockSpec((VT_ROWS, TM_PROJ), lambda i: (0, i)),
            pl.BlockSpec((IWT_ROWS, TM_PROJ), lambda i: (0, i)),
        ],
        out_shape=[
            jax.ShapeDtypeStruct((t, MAIN_WIDTH), jnp.bfloat16),
            jax.ShapeDtypeStruct((VT_ROWS, t), jnp.bfloat16),
            jax.ShapeDtypeStruct((IWT_ROWS, t), jnp.float32),
        ],
        compiler_params=pltpu.CompilerParams(dimension_semantics=("parallel",),
                                             vmem_limit_bytes=VMEM_LIMIT),
        name="proj",
    )(x2, g, w_main, w_t, cos_t, sin_t)


N_ABLK = SEQ // A_BLOCK


def _moba_kernel(q_ref, k_ref, vt_ref, o_ref):
    kmean = jnp.concatenate(
        [jnp.mean(k_ref[n * A_BLOCK:(n + 1) * A_BLOCK, :].astype(jnp.float32), axis=0, keepdims=True)
         for n in range(N_ABLK)], axis=0).astype(jnp.bfloat16)
    for own in range(N_ABLK):
        _moba_body(own, kmean, q_ref, k_ref, vt_ref, o_ref)


def _moba_body(own, kmean, q_ref, k_ref, vt_ref, o_ref):
    nk = (own + 1) * A_BLOCK
    q = q_ref[own * A_BLOCK:nk, :]
    cols = 2 * A_BLOCK
    blk_f = lax.broadcasted_iota(jnp.int32, (N_ABLK, cols), 0).astype(jnp.float32)
    kpos = lax.broadcasted_iota(jnp.int32, (A_BLOCK, cols), 0)
    qpos = lax.broadcasted_iota(jnp.int32, (A_BLOCK, cols), 1) & (A_BLOCK - 1)
    causal = kpos <= qpos
    q2 = jnp.concatenate([jnp.where(_head_mask(hh), q, jnp.zeros_like(q)) for hh in range(2)], axis=0)
    gate = _dot_nt(kmean, q2)
    gate = jnp.where(blk_f < float(own), gate, NEG_INF)
    sel = jnp.zeros(gate.shape, jnp.float32)
    for _ in range(min(A_TOPK, own)):
        top = jnp.max(gate, axis=0, keepdims=True)
        is_top = (gate == top) & (top > NEG_INF)
        idx = jnp.min(jnp.where(is_top, blk_f, float(N_ABLK)), axis=0, keepdims=True)
        pick = blk_f == idx
        sel = jnp.where(pick, 1.0, sel)
        gate = jnp.where(pick, NEG_INF, gate)
    s = _dot_nt(k_ref[0:nk, :], q2)
    parts = [jnp.where(sel[n:n + 1, :] > 0.0, s[n * A_BLOCK:(n + 1) * A_BLOCK], NEG_INF)
             for n in range(own)]
    parts.append(jnp.where(causal, s[own * A_BLOCK:], NEG_INF))
    s = jnp.concatenate(parts, axis=0)
    m = _colmax(s)
    p = jnp.exp2(s - m)
    l = _colsum(p)
    o = _dot(vt_ref[:, 0:nk], p.astype(jnp.bfloat16)) / l
    o = jnp.concatenate([o[hh * HEAD_DIM:(hh + 1) * HEAD_DIM, hh * A_BLOCK:(hh + 1) * A_BLOCK]
                         for hh in range(2)], axis=0)
    o_ref[own * A_BLOCK:nk, :] = o.T.astype(o_ref.dtype)


def _moba(main, vt, bsz):
    t = main.shape[0]
    n_pairs = A_HEADS // 2
    q_col = _MAIN_OFF["a_q"] // LANES
    k_col = _MAIN_OFF["a_k"] // LANES
    return pl.pallas_call(
        _moba_kernel,
        grid=(bsz, n_pairs),
        in_specs=[
            pl.BlockSpec((SEQ, LANES), lambda b, p: (b, q_col + p)),
            pl.BlockSpec((SEQ, LANES), lambda b, p: (b, k_col + p)),
            pl.BlockSpec((LANES, SEQ), lambda b, p: (p, b)),
        ],
        out_specs=pl.BlockSpec((SEQ, LANES), lambda b, p: (b, p)),
        out_shape=jax.ShapeDtypeStruct((t, A_WIDTH), jnp.bfloat16),
        compiler_params=pltpu.CompilerParams(
            dimension_semantics=("parallel", "parallel"), vmem_limit_bytes=VMEM_LIMIT),
        name="moba",
    )(main, main, vt)


TIE_CHUNK = 256


KV_STEP = 256


def _dsa_kernel(ikk_ref, kk_ref, vvt_ref, iq_a, iwt_a, q_a, iq_b, iwt_b, q_b, o_ref, u_a, bias_a, u_b, bias_b):
    j = pl.program_id(1)
    n_qt = SEQ // TQ_DSA
    tiles_per_step = KV_STEP // TQ_DSA
    n_var = SEQ // KV_STEP
    for c in range(n_var // 2):
        @pl.when(j // tiles_per_step == c)
        def _(c=c):
            _dsa_body((c + 1) * KV_STEP, j, ikk_ref, kk_ref, vvt_ref, iq_a, iwt_a, q_a, o_ref, u_a, bias_a)
            _dsa_body((n_var - c) * KV_STEP, n_qt - 1 - j, ikk_ref, kk_ref, vvt_ref, iq_b, iwt_b, q_b, o_ref,
                      u_b, bias_b)


def _dsa_body(nk, qt, ikk_ref, kk_ref, vvt_ref, iq_ref, iwt_ref, q_ref, o_ref, u_ref, bias_ref):
    n_keep = float(B_TOPK)
    key_pos = lax.broadcasted_iota(jnp.int32, (nk, TQ_DSA), 0)
    q_pos = qt * TQ_DSA + lax.broadcasted_iota(jnp.int32, (nk, TQ_DSA), 1)
    causal = key_pos <= q_pos

    iw = iwt_ref[...]
    iq_all = _stack_heads(iq_ref[...], B_IDX_HEADS)
    iw_all = jnp.concatenate([iw[h:h + 1, :] for h in range(B_IDX_HEADS)], axis=1)
    part = jnp.maximum(_dot_nt(ikk_ref[0:nk, :], iq_all), 0.0) * iw_all
    score = part[:, 0:TQ_DSA]
    for h in range(1, B_IDX_HEADS):
        score = score + part[:, h * TQ_DSA:(h + 1) * TQ_DSA]

    score = jnp.where(score == 0.0, 0.0, score)
    bits = pltpu.bitcast(score, jnp.int32)
    u = jnp.where(bits < 0, bits ^ jnp.int32(0x7FFFFFFF), bits)
    u_ref[0:nk, :] = jnp.where(causal, u, jnp.int32(INT_MIN))

    def bisect(i, tau):
        trial = tau ^ jnp.left_shift(jnp.int32(1), 31 - i)
        acc = jnp.zeros((LANES, TQ_DSA), jnp.float32)
        for c in range(nk // LANES):
            acc = jnp.where(u_ref[c * LANES:(c + 1) * LANES, :] >= trial, acc + 1.0, acc)
        cnt = acc.sum(axis=0, keepdims=True)
        return jnp.where(cnt >= n_keep, trial, tau)

    tau = lax.fori_loop(0, 32, bisect, jnp.full((1, TQ_DSA), INT_MIN, jnp.int32))
    bounded = tau > jnp.int32(INT_MIN)

    ge = u_ref[0:nk, :] >= jnp.maximum(tau, jnp.int32(INT_MIN + 1))
    n_ge = _colsum(jnp.where(ge, 1.0, 0.0))
    bias_ref[0:nk, :] = jnp.where(ge, 0.0, NEG_INF)
    excess = jnp.where(bounded, n_ge - n_keep, 0.0)

    @pl.when(jnp.max(excess) > 0.0)
    def _():
        n_gt = _colsum(jnp.where(u_ref[0:nk, :] > tau, 1.0, 0.0))
        need = n_keep - n_gt
        r_i = lax.broadcasted_iota(jnp.int32, (TIE_CHUNK, TIE_CHUNK), 0)
        c_i = lax.broadcasted_iota(jnp.int32, (TIE_CHUNK, TIE_CHUNK), 1)
        below = jnp.where(c_i < r_i, 1.0, 0.0).astype(jnp.bfloat16)
        seen = jnp.zeros((1, TQ_DSA), jnp.float32)
        for c in range(nk // TIE_CHUNK):
            uc = u_ref[c * TIE_CHUNK:(c + 1) * TIE_CHUNK, :]
            eq = jnp.where((uc == tau) & bounded, 1.0, 0.0)
            rank = _dot(below, eq.astype(jnp.bfloat16)) + seen
            seen = seen + jnp.sum(eq, axis=0, keepdims=True)
            keep = (uc > tau) | ((eq > 0.0) & (rank < need))
            bias_ref[c * TIE_CHUNK:(c + 1) * TIE_CHUNK, :] = jnp.where(keep, 0.0, NEG_INF)

    q_all = _stack_heads(q_ref[...], B_HEADS)
    bias = bias_ref[0:nk, :]
    s = _dot_nt(kk_ref[0:nk, :], q_all) + jnp.concatenate([bias] * B_HEADS, axis=1)
    m = _colmax(s)
    p = jnp.exp2(s - m)
    l = _colsum(p)
    o = _dot(vvt_ref[:, 0:nk], p.astype(jnp.bfloat16)) / l
    o = jnp.concatenate([o[(h % 2) * HEAD_DIM:(h % 2 + 1) * HEAD_DIM, h * TQ_DSA:(h + 1) * TQ_DSA]
                         for h in range(B_HEADS)], axis=0)
    o_ref[pl.ds(pl.multiple_of(qt * TQ_DSA, TQ_DSA), TQ_DSA), :] = o.T.astype(o_ref.dtype)


def _stack_heads(x, n_heads):
    rows = []
    for h in range(n_heads):
        blk = x[:, (h // 2) * LANES:(h // 2 + 1) * LANES]
        rows.append(jnp.where(_head_mask(h % 2), blk, jnp.zeros_like(blk)))
    return jnp.concatenate(rows, axis=0)


def _dsa(main, vt, iwt, bsz):
    t = main.shape[0]
    n_qt = SEQ // TQ_DSA

    def tile_specs(tile):
        return [
            pl.BlockSpec((TQ_DSA, 2 * LANES),
                         lambda b, i: (b * n_qt + tile(i), _MAIN_OFF["i_q"] // (2 * LANES))),
            pl.BlockSpec((IWT_ROWS, TQ_DSA), lambda b, i: (0, b * n_qt + tile(i))),
            pl.BlockSpec((TQ_DSA, B_WIDTH), lambda b, i: (b * n_qt + tile(i), _MAIN_OFF["b_q"] // B_WIDTH)),
        ]

    return pl.pallas_call(
        _dsa_kernel,
        grid=(bsz, n_qt // 2),
        in_specs=[
            pl.BlockSpec((SEQ, LANES), lambda b, i: (b, _MAIN_OFF["i_kk"] // LANES)),
            pl.BlockSpec((SEQ, LANES), lambda b, i: (b, _MAIN_OFF["b_kk"] // LANES)),
            pl.BlockSpec((LANES, SEQ), lambda b, i: (A_WIDTH // LANES, b)),
        ] + tile_specs(lambda i: i) + tile_specs(lambda i: n_qt - 1 - i),
        out_specs=pl.BlockSpec((SEQ, B_WIDTH), lambda b, i: (b, 0)),
        out_shape=jax.ShapeDtypeStruct((t, B_WIDTH), jnp.bfloat16),
        scratch_shapes=[
            pltpu.VMEM((SEQ, TQ_DSA), jnp.int32),
            pltpu.VMEM((SEQ, TQ_DSA), jnp.float32),
        ] * 2,
        compiler_params=pltpu.CompilerParams(
            dimension_semantics=("parallel", "arbitrary"), vmem_limit_bytes=VMEM_LIMIT),
        name="dsa",
    )(main, main, vt, main, iwt, main, main, iwt, main)


N_CBLK = SEQ // C_BLOCK
N_GROUPS = len(C_GROUPS)


def _dilated_kernel(*refs):
    q_in = refs[0:N_GROUPS]
    k_in = refs[N_GROUPS:2 * N_GROUPS]
    v_in = refs[2 * N_GROUPS:3 * N_GROUPS]
    o_ref = refs[3 * N_GROUPS]
    scratch = refs[3 * N_GROUPS + 1:]
    tmp_ref = scratch[0]
    dense = scratch[1:1 + 3 * (N_GROUPS - 1)]
    od_ref, ld_ref = scratch[-2 - 2 * N_GROUPS], scratch[-1 - 2 * N_GROUPS]
    o_tok = scratch[-2 * N_GROUPS:-N_GROUPS]
    l_tok = scratch[-N_GROUPS:]

    lane = lax.broadcasted_iota(jnp.int32, (1, LANES), 1)
    low = lane < HEAD_DIM
    qi = lax.broadcasted_iota(jnp.int32, (C_BLOCK, C_BLOCK), 0)
    ki = lax.broadcasted_iota(jnp.int32, (C_BLOCK, C_BLOCK), 1)
    own_ok = ki <= qi

    for g, (window, dil) in enumerate(C_GROUPS):
        assert window // dil == C_BLOCK
        n_sub = SEQ // dil
        n_blk = n_sub // C_BLOCK
        if dil == 1:
            qd, kd, vd = q_in[g], k_in[g], v_in[g]
        else:
            qd, kd, vd = dense[3 * (g - 1):3 * g]
            for src, dst in ((q_in[g], qd), (k_in[g], kd), (v_in[g], vd)):
                tmp_ref[...] = src[...].astype(jnp.float32)
                for r in range(dil):
                    dst[r * n_sub:(r + 1) * n_sub, :] = tmp_ref[pl.ds(r, n_sub, stride=dil), :].astype(dst.dtype)
        o_dst = o_tok[g] if dil == 1 else od_ref
        l_dst = l_tok[g] if dil == 1 else ld_ref

        blocked = (N_CBLK, C_BLOCK, LANES)
        q3 = qd[...].reshape(blocked)
        k3 = kd[...].reshape(blocked)
        v3 = vd[...].reshape(blocked)
        if n_blk > 1:
            pad = jnp.zeros((1, C_BLOCK, LANES), k3.dtype)
            k3 = jnp.concatenate([jnp.concatenate([pad, k3[:-1]], axis=0), k3], axis=1)
            v3 = jnp.concatenate([jnp.concatenate([pad, v3[:-1]], axis=0), v3], axis=1)
            shape = (N_CBLK, C_BLOCK, 2 * C_BLOCK)
            b_i = lax.broadcasted_iota(jnp.int32, shape, 0)
            q_i = lax.broadcasted_iota(jnp.int32, shape, 1)
            k_i = lax.broadcasted_iota(jnp.int32, shape, 2)
            lo = jnp.minimum(q_i + jnp.where((b_i & (n_blk - 1)) == 0, C_BLOCK, 0), C_BLOCK)
            bias = jnp.where(k_i >= lo, jnp.where(k_i <= q_i + C_BLOCK, 0.0, NEG_INF), NEG_INF)
        else:
            bias = jnp.where(own_ok, 0.0, NEG_INF)[None]
        o_h, l_h = [], []
        for hh in range(2):
            qh = jnp.where(_head_mask(hh)[None], q3, jnp.zeros_like(q3))
            s = jnp.einsum("bqd,bkd->bqk", qh, k3, preferred_element_type=jnp.float32) + bias
            m = jnp.max(s, axis=-1, keepdims=True)
            p = jnp.exp2(s - m)
            l = jnp.sum(p, axis=-1, keepdims=True)
            o = jnp.einsum("bqk,bkd->bqd", p.astype(jnp.bfloat16), v3, preferred_element_type=jnp.float32)
            o_h.append(o / l)
            l_h.append(m + jnp.log2(l))
        o_dst[...] = jnp.where(low[None], o_h[0], o_h[1]).reshape(SEQ, LANES)
        l_dst[...] = jnp.where(low[None], l_h[0], l_h[1]).reshape(SEQ, LANES)
        if dil > 1:
            for r in range(dil):
                o_tok[g][pl.ds(r, n_sub, stride=dil), :] = od_ref[r * n_sub:(r + 1) * n_sub, :]
                l_tok[g][pl.ds(r, n_sub, stride=dil), :] = ld_ref[r * n_sub:(r + 1) * n_sub, :]

    lses = [l_tok[g][...] for g in range(N_GROUPS)]
    top = functools.reduce(jnp.maximum, lses)
    es = [jnp.exp2(x - top) for x in lses]
    num = sum(e * o_tok[g][...] for g, e in enumerate(es))
    o_ref[...] = (num / sum(es)).astype(o_ref.dtype)


def _dilated(main, bsz):
    t = main.shape[0]
    n_pairs = C_SLOTS // 2

    def col_spec(name, g):
        base = _MAIN_OFF[name] // LANES + g * n_pairs
        return pl.BlockSpec((SEQ, LANES), lambda b, p: (b, base + p))

    in_specs = ([col_spec("c_q", g) for g in range(N_GROUPS)]
                + [col_spec("c_k", g) for g in range(N_GROUPS)]
                + [col_spec("c_v", g) for g in range(N_GROUPS)])
    scratch = ([pltpu.VMEM((SEQ, LANES), jnp.float32)]
               + [pltpu.VMEM((SEQ, LANES), jnp.bfloat16)] * (3 * (N_GROUPS - 1))
               + [pltpu.VMEM((SEQ, LANES), jnp.float32)] * (2 + 2 * N_GROUPS))
    return pl.pallas_call(
        _dilated_kernel,
        grid=(bsz, n_pairs),
        in_specs=in_specs,
        out_specs=pl.BlockSpec((SEQ, LANES), lambda b, p: (b, p)),
        out_shape=jax.ShapeDtypeStruct((t, C_WIDTH), jnp.bfloat16),
        scratch_shapes=scratch,
        compiler_params=pltpu.CompilerParams(
            dimension_semantics=("parallel", "parallel"), vmem_limit_bytes=VMEM_LIMIT),
        name="dilated",
    )(*([main] * (3 * N_GROUPS)))


def _sigmoid(x):
    return 1.0 / (1.0 + jnp.exp(-x))


def _merge_kernel(x_ref, g_ref, ya_ref, yb_ref, yc_ref, wg_ref, wa_ref, wb_ref, wc_ref, wo_ref, fg_ref,
                  o_ref, *, final):
    x = x_ref[...]
    h = _rms(x, g_ref[...]).astype(jnp.bfloat16)
    merged = jnp.zeros(x.shape, jnp.float32)
    c0 = 0
    m0 = A_WIDTH + B_WIDTH + C_WIDTH
    for i, (y_ref, w_ref) in enumerate(((ya_ref, wa_ref), (yb_ref, wb_ref), (yc_ref, wc_ref))):
        width = y_ref.shape[1]
        gate = _dot(h, wg_ref[:, c0:c0 + width])
        z = (y_ref[...].astype(jnp.float32) * (gate * _sigmoid(gate))).astype(jnp.bfloat16)
        mix = _sigmoid(_dot(h, wg_ref[:, m0 + i * D_MODEL:m0 + (i + 1) * D_MODEL]))
        merged = merged + mix * _dot(z, w_ref[...])
        c0 += width
    out = x + _dot(merged.astype(jnp.bfloat16), wo_ref[...])
    if final:
        out = _rms(out, fg_ref[...])
    o_ref[...] = out


def _merge(x2, g, ya, yb, yc, w_gate, w_a, w_b, w_c, w_o, fg, final):
    t = x2.shape[0]
    row = lambda w: pl.BlockSpec((TM_MERGE, w), lambda i: (i, 0))
    full = lambda a: pl.BlockSpec(a.shape, lambda i: (0, 0))
    return pl.pallas_call(
        functools.partial(_merge_kernel, final=final),
        grid=(t // TM_MERGE,),
        in_specs=[row(D_MODEL), full(g), row(A_WIDTH), row(B_WIDTH), row(C_WIDTH),
                  full(w_gate), full(w_a), full(w_b), full(w_c), full(w_o), full(fg)],
        out_specs=row(D_MODEL),
        out_shape=jax.ShapeDtypeStruct((t, D_MODEL), jnp.float32),
        compiler_params=pltpu.CompilerParams(dimension_semantics=("parallel",),
                                             vmem_limit_bytes=VMEM_LIMIT),
        name="merge_final" if final else "merge",
    )(x2, g, ya, yb, yc, w_gate, w_a, w_b, w_c, w_o, fg)


def _rope_tables():
    inv_freq = 1.0 / (ROPE_THETA ** (jnp.arange(0, HEAD_DIM, 2, dtype=jnp.float32) / HEAD_DIM))
    ang = jnp.arange(SEQ, dtype=jnp.float32)[:, None] * inv_freq[None, :]
    cos, sin = jnp.cos(ang), jnp.sin(ang)
    reps = LANES // HEAD_DIM
    cos_t = jnp.tile(jnp.concatenate([cos, cos], axis=1), (1, reps))
    sin_t = jnp.tile(jnp.concatenate([-sin, sin], axis=1), (1, reps))
    return cos_t, sin_t


def _split_w_in(w):
    names = ("a_q", "a_k", "a_v", "a_g", "b_q", "b_k", "b_v", "b_g", "i_q", "i_k", "i_w",
             "c_q", "c_k", "c_v", "c_g", "m_g")
    return {n: w[:, _OFF[i]:_OFF[i + 1]] for i, n in enumerate(names)}


def _column_scales():
    scale = np.ones((_OFF[-1],), np.float32)
    names = {"a_q": 0, "b_q": 4, "c_q": 11}
    for i in names.values():
        scale[_OFF[i]:_OFF[i + 1]] = HEAD_DIM ** -0.5 * LOG2_E
    scale[_OFF[10]:_OFF[11]] = (B_IDX_HEADS * B_IDX_DIM) ** -0.5
    return scale


def _layer_weights(w_bf):
    p = _split_w_in(w_bf)
    main = jnp.concatenate([
        p["a_q"], p["a_k"], p["b_q"], p["b_k"], p["b_k"],
        p["i_q"], p["i_k"], p["i_k"], p["c_q"], p["c_k"], p["c_v"]], axis=1)
    w_t = jnp.concatenate([
        p["a_v"].T, p["b_v"].T, p["b_v"].T, p["i_w"].T,
        jnp.zeros((IWT_ROWS - B_IDX_HEADS, D_MODEL), w_bf.dtype)], axis=0)
    gates = jnp.concatenate([p["a_g"], p["b_g"], p["c_g"], p["m_g"]], axis=1)
    return main, w_t, gates


def kernel(x, norm_g, w_in, w_br_a, w_br_b, w_br_c, w_out, final_norm_g):
    bsz, seq, d_model = x.shape
    assert seq == SEQ and d_model == D_MODEL
    assert MAIN_WIDTH % LANES == 0 and GATE_WIDTH == sum(IN_SPLITS[i] for i in (3, 7, 14, 15))
    cos_t, sin_t = _rope_tables()
    bf = jnp.bfloat16
    x2 = x.reshape(bsz * seq, d_model)
    fg = final_norm_g.reshape(1, d_model)
    w_in_bf = (w_in * _column_scales()).astype(bf)
    for layer in range(DEPTH):
        w_main, w_t, w_gate = _layer_weights(w_in_bf[layer])
        g = norm_g[layer].reshape(1, d_model)
        main, vt, iwt = _proj(x2, g, w_main, w_t, cos_t, sin_t)
        ya = _moba(main, vt, bsz)
        yb = _dsa(main, vt, iwt, bsz)
        yc = _dilated(main, bsz)
        x2 = _merge(x2, g, ya, yb, yc, w_gate, w_br_a[layer].astype(bf), w_br_b[layer].astype(bf),
                    w_br_c[layer].astype(bf), w_out[layer].astype(bf), fg, final=(layer == DEPTH - 1))
    return x2.reshape(bsz, seq, d_model)
```

```python
import functools

import numpy as np
import jax
import jax.numpy as jnp
from jax import lax
from jax.experimental import pallas as pl
from jax.experimental.pallas import tpu as pltpu

D_MODEL = 1024
SEQ = 2048
DEPTH = 2
HEAD_DIM = 64
ROPE_THETA = 10000.0
RMS_EPS = 1e-6
N_BRANCH = 3

A_HEADS = 6
A_WIDTH = A_HEADS * HEAD_DIM
A_BLOCK = 256
A_TOPK = 3

B_HEADS = 6
B_WIDTH = B_HEADS * HEAD_DIM
B_TOPK = 256
B_IDX_HEADS = 4
B_IDX_DIM = HEAD_DIM

C_GROUPS = ((128, 1), (512, 4), (2048, 16))
C_SLOTS = 4
C_HEADS = C_SLOTS * len(C_GROUPS)
C_WIDTH = C_SLOTS * HEAD_DIM
C_BLOCK = 128

IN_SPLITS = (A_WIDTH, A_WIDTH, A_WIDTH, A_WIDTH,
             B_WIDTH, HEAD_DIM, HEAD_DIM, B_WIDTH, B_IDX_HEADS * B_IDX_DIM, B_IDX_DIM, B_IDX_HEADS,
             C_HEADS * HEAD_DIM, C_HEADS * HEAD_DIM, C_HEADS * HEAD_DIM, C_WIDTH,
             N_BRANCH * D_MODEL)
_OFF = tuple(int(o) for o in np.cumsum((0,) + IN_SPLITS))

LANES = 128
VMEM_LIMIT = 56 * 1024 * 1024

_MAIN_GROUPS = (("a_q", A_WIDTH), ("a_k", A_WIDTH), ("b_q", B_WIDTH), ("b_kk", LANES),
                ("i_q", B_IDX_HEADS * B_IDX_DIM), ("i_kk", LANES),
                ("c_q", C_HEADS * HEAD_DIM), ("c_k", C_HEADS * HEAD_DIM), ("c_v", C_HEADS * HEAD_DIM))
_MAIN_OFF = {}
_c = 0
for _n, _w in _MAIN_GROUPS:
    _MAIN_OFF[_n] = _c
    _c += _w
MAIN_WIDTH = _c
ROPE_WIDTH = _MAIN_OFF["c_v"]
VT_ROWS = A_WIDTH + LANES
IWT_ROWS = 8
GATE_WIDTH = A_WIDTH + B_WIDTH + C_WIDTH + N_BRANCH * D_MODEL

TM_PROJ = 512
TM_MERGE = 512
TQ_DSA = 128
INT_MIN = -2 ** 31
LOG2_E = 1.4426950408889634
NEG_INF = float("-inf")


def _dot(a, b):
    return jnp.dot(a, b, preferred_element_type=jnp.float32)


def _dot_nt(a, b):
    return lax.dot_general(a, b, (((1,), (1,)), ((), ())), preferred_element_type=jnp.float32)


def _colsum(x):
    n = x.shape[0]
    part = x.reshape(n // LANES, LANES, x.shape[1]).sum(axis=0)
    return part.sum(axis=0, keepdims=True)


def _colmax(x):
    n = x.shape[0]
    part = x.reshape(n // LANES, LANES, x.shape[1]).max(axis=0)
    return part.max(axis=0, keepdims=True)


def _rms(x, g):
    return x * lax.rsqrt(jnp.mean(x * x, axis=-1, keepdims=True) + RMS_EPS) * g


def _head_mask(hh):
    lane = lax.broadcasted_iota(jnp.int32, (1, LANES), 1)
    return (lane >= hh * HEAD_DIM) & (lane < (hh + 1) * HEAD_DIM)


def _proj_kernel(x_ref, g_ref, w_ref, wt_ref, cos_ref, sin_ref, main_ref, vt_ref, iwt_ref):
    h = _rms(x_ref[...], g_ref[...]).astype(jnp.bfloat16)
    cos = cos_ref[...]
    sin = sin_ref[...]
    lane = lax.broadcasted_iota(jnp.int32, (1, LANES), 1)
    first_half = (lane % HEAD_DIM) < (HEAD_DIM // 2)
    chunk = 4 * LANES
    for c0 in range(0, MAIN_WIDTH, chunk):
        w = min(chunk, MAIN_WIDTH - c0)
        res = _dot(h, w_ref[:, c0:c0 + w])
        for j in range(0, w, LANES):
            blk = res[:, j:j + LANES]
            if c0 + j < ROPE_WIDTH:
                partner = jnp.where(first_half, pltpu.roll(blk, LANES - HEAD_DIM // 2, 1),
                                    pltpu.roll(blk, HEAD_DIM // 2, 1))
                blk = blk * cos + partner * sin
            main_ref[:, c0 + j:c0 + j + LANES] = blk.astype(main_ref.dtype)
    rt = _dot_nt(wt_ref[...], h)
    vt_ref[...] = rt[:VT_ROWS].astype(vt_ref.dtype)
    iwt_ref[...] = rt[VT_ROWS:]


def _proj(x2, g, w_main, w_t, cos_t, sin_t):
    t = x2.shape[0]
    n_rope_tiles = SEQ // TM_PROJ
    return pl.pallas_call(
        _proj_kernel,
        grid=(t // TM_PROJ,),
        in_specs=[
            pl.BlockSpec((TM_PROJ, D_MODEL), lambda i: (i, 0)),
            pl.BlockSpec((1, D_MODEL), lambda i: (0, 0)),
            pl.BlockSpec((D_MODEL, MAIN_WIDTH), lambda i: (0, 0)),
            pl.BlockSpec((VT_ROWS + IWT_ROWS, D_MODEL), lambda i: (0, 0)),
            pl.BlockSpec((TM_PROJ, LANES), lambda i: (i % n_rope_tiles, 0)),
            pl.BlockSpec((TM_PROJ, LANES), lambda i: (i % n_rope_tiles, 0)),
        ],
        out_specs=[
            pl.BlockSpec((TM_PROJ, MAIN_WIDTH), lambda i: (i, 0)),
            pl.BlockSpec((VT_ROWS, TM_PROJ), lambda i: (0, i)),
            pl.BlockSpec((IWT_ROWS, TM_PROJ), lambda i: (0, i)),
        ],
        out_shape=[
            jax.ShapeDtypeStruct((t, MAIN_WIDTH), jnp.bfloat16),
            jax.ShapeDtypeStruct((VT_ROWS, t), jnp.bfloat16),
            jax.ShapeDtypeStruct((IWT_ROWS, t), jnp.float32),
        ],
        compiler_params=pltpu.CompilerParams(dimension_semantics=("parallel",),
                                             vmem_limit_bytes=VMEM_LIMIT),
        name="proj",
    )(x2, g, w_main, w_t, cos_t, sin_t)


N_ABLK = SEQ // A_BLOCK


def _moba_kernel(q_ref, k_ref, vt_ref, o_ref):
    kmean = jnp.concatenate(
        [jnp.mean(k_ref[n * A_BLOCK:(n + 1) * A_BLOCK, :].astype(jnp.float32), axis=0, keepdims=True)
         for n in range(N_ABLK)], axis=0).astype(jnp.bfloat16)
    for own in range(N_ABLK):
        _moba_body(own, kmean, q_ref, k_ref, vt_ref, o_ref)


def _moba_body(own, kmean, q_ref, k_ref, vt_ref, o_ref):
    nk = (own + 1) * A_BLOCK
    q = q_ref[own * A_BLOCK:nk, :]
    cols = 2 * A_BLOCK
    blk_f = lax.broadcasted_iota(jnp.int32, (N_ABLK, cols), 0).astype(jnp.float32)
    kpos = lax.broadcasted_iota(jnp.int32, (A_BLOCK, cols), 0)
    qpos = lax.broadcasted_iota(jnp.int32, (A_BLOCK, cols), 1) & (A_BLOCK - 1)
    causal = kpos <= qpos
    q2 = jnp.concatenate([jnp.where(_head_mask(hh), q, jnp.zeros_like(q)) for hh in range(2)], axis=0)
    gate = _dot_nt(kmean, q2)
    gate = jnp.where(blk_f < float(own), gate, NEG_INF)
    sel = jnp.zeros(gate.shape, jnp.float32)
    for _ in range(min(A_TOPK, own)):
        top = jnp.max(gate, axis=0, keepdims=True)
        is_top = (gate == top) & (top > NEG_INF)
        idx = jnp.min(jnp.where(is_top, blk_f, float(N_ABLK)), axis=0, keepdims=True)
        pick = blk_f == idx
        sel = jnp.where(pick, 1.0, sel)
        gate = jnp.where(pick, NEG_INF, gate)
    s = _dot_nt(k_ref[0:nk, :], q2)
    parts = [jnp.where(sel[n:n + 1, :] > 0.0, s[n * A_BLOCK:(n + 1) * A_BLOCK], NEG_INF)
             for n in range(own)]
    parts.append(jnp.where(causal, s[own * A_BLOCK:], NEG_INF))
    s = jnp.concatenate(parts, axis=0)
    m = _colmax(s)
    p = jnp.exp2(s - m)
    l = _colsum(p)
    o = _dot(vt_ref[:, 0:nk], p.astype(jnp.bfloat16)) / l
    o = jnp.concatenate([o[hh * HEAD_DIM:(hh + 1) * HEAD_DIM, hh * A_BLOCK:(hh + 1) * A_BLOCK]
                         for hh in range(2)], axis=0)
    o_ref[own * A_BLOCK:nk, :] = o.T.astype(o_ref.dtype)


def _moba(main, vt, bsz):
    t = main.shape[0]
    n_pairs = A_HEADS // 2
    q_col = _MAIN_OFF["a_q"] // LANES
    k_col = _MAIN_OFF["a_k"] // LANES
    return pl.pallas_call(
        _moba_kernel,
        grid=(bsz, n_pairs),
        in_specs=[
            pl.BlockSpec((SEQ, LANES), lambda b, p: (b, q_col + p)),
            pl.BlockSpec((SEQ, LANES), lambda b, p: (b, k_col + p)),
            pl.BlockSpec((LANES, SEQ), lambda b, p: (p, b)),
        ],
        out_specs=pl.BlockSpec((SEQ, LANES), lambda b, p: (b, p)),
        out_shape=jax.ShapeDtypeStruct((t, A_WIDTH), jnp.bfloat16),
        compiler_params=pltpu.CompilerParams(
            dimension_semantics=("parallel", "parallel"), vmem_limit_bytes=VMEM_LIMIT),
        name="moba",
    )(main, main, vt)


TIE_CHUNK = 256


KV_STEP = 256


def _dsa_kernel(ikk_ref, kk_ref, vvt_ref, iq_a, iwt_a, q_a, iq_b, iwt_b, q_b, o_ref, u_a, bias_a, u_b, bias_b):
    j = pl.program_id(1)
    n_qt = SEQ // TQ_DSA
    tiles_per_step = KV_STEP // TQ_DSA
    n_var = SEQ // KV_STEP
    for c in range(n_var // 2):
        @pl.when(j // tiles_per_step == c)
        def _(c=c):
            _dsa_body((c + 1) * KV_STEP, j, ikk_ref, kk_ref, vvt_ref, iq_a, iwt_a, q_a, o_ref, u_a, bias_a)
            _dsa_body((n_var - c) * KV_STEP, n_qt - 1 - j, ikk_ref, kk_ref, vvt_ref, iq_b, iwt_b, q_b, o_ref,
                      u_b, bias_b)


def _dsa_body(nk, qt, ikk_ref, kk_ref, vvt_ref, iq_ref, iwt_ref, q_ref, o_ref, u_ref, bias_ref):
    key_pos = lax.broadcasted_iota(jnp.int32, (nk, TQ_DSA), 0)
    q_pos = qt * TQ_DSA + lax.broadcasted_iota(jnp.int32, (nk, TQ_DSA), 1)
    causal = key_pos <= q_pos
    if nk <= B_TOPK:
        bias_ref[0:nk, :] = jnp.where(causal, 0.0, NEG_INF)
    else:
        _dsa_select(nk, causal, ikk_ref, iq_ref, iwt_ref, u_ref, bias_ref)

    q_all = _stack_heads(q_ref[...], B_HEADS)
    bias = bias_ref[0:nk, :]
    s = _dot_nt(kk_ref[0:nk, :], q_all) + jnp.concatenate([bias] * B_HEADS, axis=1)
    m = _colmax(s)
    p = jnp.exp2(s - m)
    l = _colsum(p)
    o = _dot(vvt_ref[:, 0:nk], p.astype(jnp.bfloat16)) / l
    o = jnp.concatenate([o[(h % 2) * HEAD_DIM:(h % 2 + 1) * HEAD_DIM, h * TQ_DSA:(h + 1) * TQ_DSA]
                         for h in range(B_HEADS)], axis=0)
    o_ref[pl.ds(pl.multiple_of(qt * TQ_DSA, TQ_DSA), TQ_DSA), :] = o.T.astype(o_ref.dtype)


def _dsa_select(nk, causal, ikk_ref, iq_ref, iwt_ref, u_ref, bias_ref):
    n_keep = float(B_TOPK)
    iw = iwt_ref[...]
    iq_all = _stack_heads(iq_ref[...], B_IDX_HEADS)
    iw_all = jnp.concatenate([iw[h:h + 1, :] for h in range(B_IDX_HEADS)], axis=1)
    part = jnp.maximum(_dot_nt(ikk_ref[0:nk, :], iq_all), 0.0) * iw_all
    score = part[:, 0:TQ_DSA]
    for h in range(1, B_IDX_HEADS):
        score = score + part[:, h * TQ_DSA:(h + 1) * TQ_DSA]

    score = jnp.where(score == 0.0, 0.0, score)
    bits = pltpu.bitcast(score, jnp.int32)
    u = jnp.where(bits < 0, bits ^ jnp.int32(0x7FFFFFFF), bits)
    u_ref[0:nk, :] = jnp.where(causal, u, jnp.int32(INT_MIN))

    def bisect(i, carry):
        tau, n_ge = carry
        trial = tau ^ jnp.left_shift(jnp.int32(1), 31 - i)
        acc = jnp.zeros((LANES, TQ_DSA), jnp.float32)
        for c in range(nk // LANES):
            acc = jnp.where(u_ref[c * LANES:(c + 1) * LANES, :] >= trial, acc + 1.0, acc)
        cnt = acc.sum(axis=0, keepdims=True)
        ok = cnt >= n_keep
        return jnp.where(ok, trial, tau), jnp.where(ok, cnt, n_ge)

    tau, n_ge = lax.fori_loop(0, 32, bisect, (jnp.full((1, TQ_DSA), INT_MIN, jnp.int32),
                                              jnp.full((1, TQ_DSA), float(nk), jnp.float32)))
    far = 1e9
    excess = jnp.where(tau > jnp.int32(INT_MIN), n_ge - n_keep, far)
    r_i = lax.broadcasted_iota(jnp.int32, (TIE_CHUNK, TIE_CHUNK), 0)
    c_i = lax.broadcasted_iota(jnp.int32, (TIE_CHUNK, TIE_CHUNK), 1)
    above = jnp.where(c_i > r_i, 1.0, 0.0).astype(jnp.bfloat16)
    later = jnp.zeros((1, TQ_DSA), jnp.float32)
    for c in reversed(range(nk // TIE_CHUNK)):
        uc = u_ref[c * TIE_CHUNK:(c + 1) * TIE_CHUNK, :]
        eq = uc == tau
        eq_f = jnp.where(eq, 1.0, 0.0)
        after = _dot(above, eq_f.astype(jnp.bfloat16)) + later
        later = later + _colsum(eq_f)
        slack = jnp.where(uc > tau, far, jnp.where(eq, after, -1.0))
        bias_ref[c * TIE_CHUNK:(c + 1) * TIE_CHUNK, :] = jnp.where(slack >= excess, 0.0, NEG_INF)


def _stack_heads(x, n_heads):
    rows = []
    for h in range(n_heads):
        blk = x[:, (h // 2) * LANES:(h // 2 + 1) * LANES]
        rows.append(jnp.where(_head_mask(h % 2), blk, jnp.zeros_like(blk)))
    return jnp.concatenate(rows, axis=0)


def _dsa(main, vt, iwt, bsz):
    t = main.shape[0]
    n_qt = SEQ // TQ_DSA

    def tile_specs(tile):
        return [
            pl.BlockSpec((TQ_DSA, 2 * LANES),
                         lambda b, i: (b * n_qt + tile(i), _MAIN_OFF["i_q"] // (2 * LANES))),
            pl.BlockSpec((IWT_ROWS, TQ_DSA), lambda b, i: (0, b * n_qt + tile(i))),
            pl.BlockSpec((TQ_DSA, B_WIDTH), lambda b, i: (b * n_qt + tile(i), _MAIN_OFF["b_q"] // B_WIDTH)),
        ]

    return pl.pallas_call(
        _dsa_kernel,
        grid=(bsz, n_qt // 2),
        in_specs=[
            pl.BlockSpec((SEQ, LANES), lambda b, i: (b, _MAIN_OFF["i_kk"] // LANES)),
            pl.BlockSpec((SEQ, LANES), lambda b, i: (b, _MAIN_OFF["b_kk"] // LANES)),
            pl.BlockSpec((LANES, SEQ), lambda b, i: (A_WIDTH // LANES, b)),
        ] + tile_specs(lambda i: i) + tile_specs(lambda i: n_qt - 1 - i),
        out_specs=pl.BlockSpec((SEQ, B_WIDTH), lambda b, i: (b, 0)),
        out_shape=jax.ShapeDtypeStruct((t, B_WIDTH), jnp.bfloat16),
        scratch_shapes=[
            pltpu.VMEM((SEQ, TQ_DSA), jnp.int32),
            pltpu.VMEM((SEQ, TQ_DSA), jnp.float32),
        ] * 2,
        compiler_params=pltpu.CompilerParams(
            dimension_semantics=("parallel", "arbitrary"), vmem_limit_bytes=VMEM_LIMIT),
        name="dsa",
    )(main, main, vt, main, iwt, main, main, iwt, main)


N_CBLK = SEQ // C_BLOCK
N_GROUPS = len(C_GROUPS)


def _dilated_kernel(*refs):
    q_in = refs[0:N_GROUPS]
    k_in = refs[N_GROUPS:2 * N_GROUPS]
    v_in = refs[2 * N_GROUPS:3 * N_GROUPS]
    o_ref = refs[3 * N_GROUPS]
    scratch = refs[3 * N_GROUPS + 1:]
    tmp_ref = scratch[0]
    dense = scratch[1:1 + 3 * (N_GROUPS - 1)]
    od_ref, ld_ref = scratch[-2 - 2 * N_GROUPS], scratch[-1 - 2 * N_GROUPS]
    o_tok = scratch[-2 * N_GROUPS:-N_GROUPS]
    l_tok = scratch[-N_GROUPS:]

    lane = lax.broadcasted_iota(jnp.int32, (1, LANES), 1)
    low = lane < HEAD_DIM
    qi = lax.broadcasted_iota(jnp.int32, (C_BLOCK, C_BLOCK), 0)
    ki = lax.broadcasted_iota(jnp.int32, (C_BLOCK, C_BLOCK), 1)
    own_ok = ki <= qi

    for g, (window, dil) in enumerate(C_GROUPS):
        assert window // dil == C_BLOCK
        n_sub = SEQ // dil
        n_blk = n_sub // C_BLOCK
        if dil == 1:
            qd, kd, vd = q_in[g], k_in[g], v_in[g]
        else:
            qd, kd, vd = dense[3 * (g - 1):3 * g]
            for src, dst in ((q_in[g], qd), (k_in[g], kd), (v_in[g], vd)):
                tmp_ref[...] = src[...].astype(jnp.float32)
                for r in range(dil):
                    dst[r * n_sub:(r + 1) * n_sub, :] = tmp_ref[pl.ds(r, n_sub, stride=dil), :].astype(dst.dtype)
        o_dst = o_tok[g] if dil == 1 else od_ref
        l_dst = l_tok[g] if dil == 1 else ld_ref

        blocked = (N_CBLK, C_BLOCK, LANES)
        q3 = qd[...].reshape(blocked)
        k3 = kd[...].reshape(blocked)
        v3 = vd[...].reshape(blocked)
        if n_blk > 1:
            pad = jnp.zeros((1, C_BLOCK, LANES), k3.dtype)
            k3 = jnp.concatenate([jnp.concatenate([pad, k3[:-1]], axis=0), k3], axis=1)
            v3 = jnp.concatenate([jnp.concatenate([pad, v3[:-1]], axis=0), v3], axis=1)
            shape = (N_CBLK, C_BLOCK, 2 * C_BLOCK)
            b_i = lax.broadcasted_iota(jnp.int32, shape, 0)
            q_i = lax.broadcasted_iota(jnp.int32, shape, 1)
            k_i = lax.broadcasted_iota(jnp.int32, shape, 2)
            lo = jnp.minimum(q_i + jnp.where((b_i & (n_blk - 1)) == 0, C_BLOCK, 0), C_BLOCK)
            bias = jnp.where(k_i >= lo, jnp.where(k_i <= q_i + C_BLOCK, 0.0, NEG_INF), NEG_INF)
        else:
            bias = jnp.where(own_ok, 0.0, NEG_INF)[None]
        o_h, l_h = [], []
        for hh in range(2):
            qh = jnp.where(_head_mask(hh)[None], q3, jnp.zeros_like(q3))
            s = jnp.einsum("bqd,bkd->bqk", qh, k3, preferred_element_type=jnp.float32) + bias
            m = jnp.max(s, axis=-1, keepdims=True)
            p = jnp.exp2(s - m)
            l = jnp.sum(p, axis=-1, keepdims=True)
            o = jnp.einsum("bqk,bkd->bqd", p.astype(jnp.bfloat16), v3, preferred_element_type=jnp.float32)
            o_h.append(o / l)
            l_h.append(m + jnp.log2(l))
        o_dst[...] = jnp.where(low[None], o_h[0], o_h[1]).reshape(SEQ, LANES)
        l_dst[...] = jnp.where(low[None], l_h[0], l_h[1]).reshape(SEQ, LANES)
        if dil > 1:
            for r in range(dil):
                o_tok[g][pl.ds(r, n_sub, stride=dil), :] = od_ref[r * n_sub:(r + 1) * n_sub, :]
                l_tok[g][pl.ds(r, n_sub, stride=dil), :] = ld_ref[r * n_sub:(r + 1) * n_sub, :]

    lses = [l_tok[g][...] for g in range(N_GROUPS)]
    top = functools.reduce(jnp.maximum, lses)
    es = [jnp.exp2(x - top) for x in lses]
    num = sum(e * o_tok[g][...] for g, e in enumerate(es))
    o_ref[...] = (num / sum(es)).astype(o_ref.dtype)


def _dilated(main, bsz):
    t = main.shape[0]
    n_pairs = C_SLOTS // 2

    def col_spec(name, g):
        base = _MAIN_OFF[name] // LANES + g * n_pairs
        return pl.BlockSpec((SEQ, LANES), lambda b, p: (b, base + p))

    in_specs = ([col_spec("c_q", g) for g in range(N_GROUPS)]
                + [col_spec("c_k", g) for g in range(N_GROUPS)]
                + [col_spec("c_v", g) for g in range(N_GROUPS)])
    scratch = ([pltpu.VMEM((SEQ, LANES), jnp.float32)]
               + [pltpu.VMEM((SEQ, LANES), jnp.bfloat16)] * (3 * (N_GROUPS - 1))
               + [pltpu.VMEM((SEQ, LANES), jnp.float32)] * (2 + 2 * N_GROUPS))
    return pl.pallas_call(
        _dilated_kernel,
        grid=(bsz, n_pairs),
        in_specs=in_specs,
        out_specs=pl.BlockSpec((SEQ, LANES), lambda b, p: (b, p)),
        out_shape=jax.ShapeDtypeStruct((t, C_WIDTH), jnp.bfloat16),
        scratch_shapes=scratch,
        compiler_params=pltpu.CompilerParams(
            dimension_semantics=("parallel", "parallel"), vmem_limit_bytes=VMEM_LIMIT),
        name="dilated",
    )(*([main] * (3 * N_GROUPS)))


def _sigmoid(x):
    return 1.0 / (1.0 + jnp.exp(-x))


def _merge_kernel(x_ref, g_ref, ya_ref, yb_ref, yc_ref, wg_ref, wa_ref, wb_ref, wc_ref, wo_ref, fg_ref,
                  o_ref, *, final):
    x = x_ref[...]
    h = _rms(x, g_ref[...]).astype(jnp.bfloat16)
    merged = jnp.zeros(x.shape, jnp.float32)
    c0 = 0
    m0 = A_WIDTH + B_WIDTH + C_WIDTH
    for i, (y_ref, w_ref) in enumerate(((ya_ref, wa_ref), (yb_ref, wb_ref), (yc_ref, wc_ref))):
        width = y_ref.shape[1]
        gate = _dot(h, wg_ref[:, c0:c0 + width])
        z = (y_ref[...].astype(jnp.float32) * (gate * _sigmoid(gate))).astype(jnp.bfloat16)
        mix = _sigmoid(_dot(h, wg_ref[:, m0 + i * D_MODEL:m0 + (i + 1) * D_MODEL]))
        merged = merged + mix * _dot(z, w_ref[...])
        c0 += width
    out = x + _dot(merged.astype(jnp.bfloat16), wo_ref[...])
    if final:
        out = _rms(out, fg_ref[...])
    o_ref[...] = out


def _merge(x2, g, ya, yb, yc, w_gate, w_a, w_b, w_c, w_o, fg, final):
    t = x2.shape[0]
    row = lambda w: pl.BlockSpec((TM_MERGE, w), lambda i: (i, 0))
    full = lambda a: pl.BlockSpec(a.shape, lambda i: (0, 0))
    return pl.pallas_call(
        functools.partial(_merge_kernel, final=final),
        grid=(t // TM_MERGE,),
        in_specs=[row(D_MODEL), full(g), row(A_WIDTH), row(B_WIDTH), row(C_WIDTH),
                  full(w_gate), full(w_a), full(w_b), full(w_c), full(w_o), full(fg)],
        out_specs=row(D_MODEL),
        out_shape=jax.ShapeDtypeStruct((t, D_MODEL), jnp.float32),
        compiler_params=pltpu.CompilerParams(dimension_semantics=("parallel",),
                                             vmem_limit_bytes=VMEM_LIMIT),
        name="merge_final" if final else "merge",
    )(x2, g, ya, yb, yc, w_gate, w_a, w_b, w_c, w_o, fg)


def _rope_tables():
    inv_freq = 1.0 / (ROPE_THETA ** (jnp.arange(0, HEAD_DIM, 2, dtype=jnp.float32) / HEAD_DIM))
    ang = jnp.arange(SEQ, dtype=jnp.float32)[:, None] * inv_freq[None, :]
    cos, sin = jnp.cos(ang), jnp.sin(ang)
    reps = LANES // HEAD_DIM
    cos_t = jnp.tile(jnp.concatenate([cos, cos], axis=1), (1, reps))
    sin_t = jnp.tile(jnp.concatenate([-sin, sin], axis=1), (1, reps))
    return cos_t, sin_t


def _split_w_in(w):
    names = ("a_q", "a_k", "a_v", "a_g", "b_q", "b_k", "b_v", "b_g", "i_q", "i_k", "i_w",
             "c_q", "c_k", "c_v", "c_g", "m_g")
    return {n: w[:, _OFF[i]:_OFF[i + 1]] for i, n in enumerate(names)}


def _column_scales():
    scale = np.ones((_OFF[-1],), np.float32)
    names = {"a_q": 0, "b_q": 4, "c_q": 11}
    for i in names.values():
        scale[_OFF[i]:_OFF[i + 1]] = HEAD_DIM ** -0.5 * LOG2_E
    scale[_OFF[10]:_OFF[11]] = (B_IDX_HEADS * B_IDX_DIM) ** -0.5
    return scale


def _layer_weights(w_bf):
    p = _split_w_in(w_bf)
    main = jnp.concatenate([
        p["a_q"], p["a_k"], p["b_q"], p["b_k"], p["b_k"],
        p["i_q"], p["i_k"], p["i_k"], p["c_q"], p["c_k"], p["c_v"]], axis=1)
    w_t = jnp.concatenate([
        p["a_v"].T, p["b_v"].T, p["b_v"].T, p["i_w"].T,
        jnp.zeros((IWT_ROWS - B_IDX_HEADS, D_MODEL), w_bf.dtype)], axis=0)
    gates = jnp.concatenate([p["a_g"], p["b_g"], p["c_g"], p["m_g"]], axis=1)
    return main, w_t, gates


def kernel(x, norm_g, w_in, w_br_a, w_br_b, w_br_c, w_out, final_norm_g):
    bsz, seq, d_model = x.shape
    assert seq == SEQ and d_model == D_MODEL
    assert MAIN_WIDTH % LANES == 0 and GATE_WIDTH == sum(IN_SPLITS[i] for i in (3, 7, 14, 15))
    cos_t, sin_t = _rope_tables()
    bf = jnp.bfloat16
    x2 = x.reshape(bsz * seq, d_model)
    fg = final_norm_g.reshape(1, d_model)
    w_in_bf = (w_in * _column_scales()).astype(bf)
    for layer in range(DEPTH):
        w_main, w_t, w_gate = _layer_weights(w_in_bf[layer])
        g = norm_g[layer].reshape(1, d_model)
        main, vt, iwt = _proj(x2, g, w_main, w_t, cos_t, sin_t)
        ya = _moba(main, vt, bsz)
        yb = _dsa(main, vt, iwt, bsz)
        yc = _dilated(main, bsz)
        x2 = _merge(x2, g, ya, yb, yc, w_gate, w_br_a[layer].astype(bf), w_br_b[layer].astype(bf),
                    w_br_c[layer].astype(bf), w_out[layer].astype(bf), fg, final=(layer == DEPTH - 1))
    return x2.reshape(bsz, seq, d_model)
```

```python
import functools

import numpy as np
import jax
import jax.numpy as jnp
from jax import lax
from jax.experimental import pallas as pl
from jax.experimental.pallas import tpu as pltpu

D_MODEL = 1024
SEQ = 2048
DEPTH = 2
HEAD_DIM = 64
ROPE_THETA = 10000.0
RMS_EPS = 1e-6
N_BRANCH = 3

A_HEADS = 6
A_WIDTH = A_HEADS * HEAD_DIM
A_BLOCK = 256
A_TOPK = 3

B_HEADS = 6
B_WIDTH = B_HEADS * HEAD_DIM
B_TOPK = 256
B_IDX_HEADS = 4
B_IDX_DIM = HEAD_DIM

C_GROUPS = ((128, 1), (512, 4), (2048, 16))
C_SLOTS = 4
C_HEADS = C_SLOTS * len(C_GROUPS)
C_WIDTH = C_SLOTS * HEAD_DIM
C_BLOCK = 128

IN_SPLITS = (A_WIDTH, A_WIDTH, A_WIDTH, A_WIDTH,
             B_WIDTH, HEAD_DIM, HEAD_DIM, B_WIDTH, B_IDX_HEADS * B_IDX_DIM, B_IDX_DIM, B_IDX_HEADS,
             C_HEADS * HEAD_DIM, C_HEADS * HEAD_DIM, C_HEADS * HEAD_DIM, C_WIDTH,
             N_BRANCH * D_MODEL)
_OFF = tuple(int(o) for o in np.cumsum((0,) + IN_SPLITS))

LANES = 128
VMEM_LIMIT = 56 * 1024 * 1024

_MAIN_GROUPS = (("a_q", A_WIDTH), ("a_k", A_WIDTH), ("b_q", B_WIDTH), ("b_kk", LANES),
                ("i_q", B_IDX_HEADS * B_IDX_DIM), ("i_kk", LANES),
                ("c_q", C_HEADS * HEAD_DIM), ("c_k", C_HEADS * HEAD_DIM), ("c_v", C_HEADS * HEAD_DIM))
_MAIN_OFF = {}
_c = 0
for _n, _w in _MAIN_GROUPS:
    _MAIN_OFF[_n] = _c
    _c += _w
MAIN_WIDTH = _c
ROPE_WIDTH = _MAIN_OFF["c_v"]
VT_ROWS = A_WIDTH + LANES
IWT_ROWS = 8
GATE_WIDTH = A_WIDTH + B_WIDTH + C_WIDTH + N_BRANCH * D_MODEL

TM_PROJ = 1024
TM_MERGE = 1024
TQ_DSA = 128
INT_MIN = -2 ** 31
LOG2_E = 1.4426950408889634
NEG_INF = float("-inf")


def _dot(a, b):
    return jnp.dot(a, b, preferred_element_type=jnp.float32)


def _dot_nt(a, b):
    return lax.dot_general(a, b, (((1,), (1,)), ((), ())), preferred_element_type=jnp.float32)


def _colsum(x):
    n = x.shape[0]
    part = x.reshape(n // LANES, LANES, x.shape[1]).sum(axis=0)
    return part.sum(axis=0, keepdims=True)


def _colmax(x):
    n = x.shape[0]
    part = x.reshape(n // LANES, LANES, x.shape[1]).max(axis=0)
    return part.max(axis=0, keepdims=True)


def _rms(x, g):
    return x * lax.rsqrt(jnp.mean(x * x, axis=-1, keepdims=True) + RMS_EPS) * g


def _head_mask(hh):
    lane = lax.broadcasted_iota(jnp.int32, (1, LANES), 1)
    return (lane >= hh * HEAD_DIM) & (lane < (hh + 1) * HEAD_DIM)


def _proj_kernel(x_ref, g_ref, w_ref, wt_ref, cos_ref, sin_ref, main_ref, vt_ref, iwt_ref):
    h = _rms(x_ref[...], g_ref[...]).astype(jnp.bfloat16)
    cos = cos_ref[...]
    sin = sin_ref[...]
    lane = lax.broadcasted_iota(jnp.int32, (1, LANES), 1)
    first_half = (lane % HEAD_DIM) < (HEAD_DIM // 2)
    chunk = 4 * LANES
    for c0 in range(0, MAIN_WIDTH, chunk):
        w = min(chunk, MAIN_WIDTH - c0)
        res = _dot(h, w_ref[:, c0:c0 + w])
        for j in range(0, w, LANES):
            blk = res[:, j:j + LANES]
            if c0 + j < ROPE_WIDTH:
                partner = jnp.where(first_half, pltpu.roll(blk, LANES - HEAD_DIM // 2, 1),
                                    pltpu.roll(blk, HEAD_DIM // 2, 1))
                blk = blk * cos + partner * sin
            main_ref[:, c0 + j:c0 + j + LANES] = blk.astype(main_ref.dtype)
    rt = _dot_nt(wt_ref[...], h)
    vt_ref[...] = rt[:VT_ROWS].astype(vt_ref.dtype)
    iwt_ref[...] = rt[VT_ROWS:]


def _proj(x2, g, w_main, w_t, cos_t, sin_t):
    t = x2.shape[0]
    n_rope_tiles = SEQ // TM_PROJ
    return pl.pallas_call(
        _proj_kernel,
        grid=(t // TM_PROJ,),
        in_specs=[
            pl.BlockSpec((TM_PROJ, D_MODEL), lambda i: (i, 0)),
            pl.BlockSpec((1, D_MODEL), lambda i: (0, 0), pipeline_mode=pl.Buffered(1)),
            pl.BlockSpec((D_MODEL, MAIN_WIDTH), lambda i: (0, 0), pipeline_mode=pl.Buffered(1)),
            pl.BlockSpec((VT_ROWS + IWT_ROWS, D_MODEL), lambda i: (0, 0), pipeline_mode=pl.Buffered(1)),
            pl.BlockSpec((TM_PROJ, LANES), lambda i: (i % n_rope_tiles, 0)),
            pl.BlockSpec((TM_PROJ, LANES), lambda i: (i % n_rope_tiles, 0)),
        ],
        out_specs=[
            pl.BlockSpec((TM_PROJ, MAIN_WIDTH), lambda i: (i, 0)),
            pl.BlockSpec((VT_ROWS, TM_PROJ), lambda i: (0, i)),
            pl.BlockSpec((IWT_ROWS, TM_PROJ), lambda i: (0, i)),
        ],
        out_shape=[
            jax.ShapeDtypeStruct((t, MAIN_WIDTH), jnp.bfloat16),
            jax.ShapeDtypeStruct((VT_ROWS, t), jnp.bfloat16),
            jax.ShapeDtypeStruct((IWT_ROWS, t), jnp.float32),
        ],
        compiler_params=pltpu.CompilerParams(dimension_semantics=("parallel",),
                                             vmem_limit_bytes=VMEM_LIMIT),
        name="proj",
    )(x2, g, w_main, w_t, cos_t, sin_t)


N_ABLK = SEQ // A_BLOCK


def _moba_kernel(q_ref, k_ref, vt_ref, o_ref):
    kmean = jnp.concatenate(
        [jnp.mean(k_ref[n * A_BLOCK:(n + 1) * A_BLOCK, :].astype(jnp.float32), axis=0, keepdims=True)
         for n in range(N_ABLK)], axis=0).astype(jnp.bfloat16)
    for own in range(N_ABLK):
        _moba_body(own, kmean, q_ref, k_ref, vt_ref, o_ref)


def _moba_body(own, kmean, q_ref, k_ref, vt_ref, o_ref):
    nk = (own + 1) * A_BLOCK
    q = q_ref[own * A_BLOCK:nk, :]
    cols = 2 * A_BLOCK
    blk_f = lax.broadcasted_iota(jnp.int32, (N_ABLK, cols), 0).astype(jnp.float32)
    kpos = lax.broadcasted_iota(jnp.int32, (A_BLOCK, cols), 0)
    qpos = lax.broadcasted_iota(jnp.int32, (A_BLOCK, cols), 1) & (A_BLOCK - 1)
    causal = kpos <= qpos
    q2 = jnp.concatenate([jnp.where(_head_mask(hh), q, jnp.zeros_like(q)) for hh in range(2)], axis=0)
    gate = _dot_nt(kmean, q2)
    gate = jnp.where(blk_f < float(own), gate, NEG_INF)
    sel = jnp.zeros(gate.shape, jnp.float32)
    for _ in range(min(A_TOPK, own)):
        top = jnp.max(gate, axis=0, keepdims=True)
        is_top = (gate == top) & (top > NEG_INF)
        idx = jnp.min(jnp.where(is_top, blk_f, float(N_ABLK)), axis=0, keepdims=True)
        pick = blk_f == idx
        sel = jnp.where(pick, 1.0, sel)
        gate = jnp.where(pick, NEG_INF, gate)
    s = _dot_nt(k_ref[0:nk, :], q2)
    parts = [jnp.where(sel[n:n + 1, :] > 0.0, s[n * A_BLOCK:(n + 1) * A_BLOCK], NEG_INF)
             for n in range(own)]
    parts.append(jnp.where(causal, s[own * A_BLOCK:], NEG_INF))
    s = jnp.concatenate(parts, axis=0)
    m = _colmax(s)
    p = jnp.exp2(s - m)
    l = _colsum(p)
    o = _dot(vt_ref[:, 0:nk], p.astype(jnp.bfloat16)) / l
    o = jnp.concatenate([o[hh * HEAD_DIM:(hh + 1) * HEAD_DIM, hh * A_BLOCK:(hh + 1) * A_BLOCK]
                         for hh in range(2)], axis=0)
    o_ref[own * A_BLOCK:nk, :] = o.T.astype(o_ref.dtype)


def _moba(main, vt, bsz):
    t = main.shape[0]
    n_pairs = A_HEADS // 2
    q_col = _MAIN_OFF["a_q"] // LANES
    k_col = _MAIN_OFF["a_k"] // LANES
    return pl.pallas_call(
        _moba_kernel,
        grid=(bsz, n_pairs),
        in_specs=[
            pl.BlockSpec((SEQ, LANES), lambda b, p: (b, q_col + p)),
            pl.BlockSpec((SEQ, LANES), lambda b, p: (b, k_col + p)),
            pl.BlockSpec((LANES, SEQ), lambda b, p: (p, b)),
        ],
        out_specs=pl.BlockSpec((SEQ, LANES), lambda b, p: (b, p)),
        out_shape=jax.ShapeDtypeStruct((t, A_WIDTH), jnp.bfloat16),
        compiler_params=pltpu.CompilerParams(
            dimension_semantics=("parallel", "parallel"), vmem_limit_bytes=VMEM_LIMIT),
        name="moba",
    )(main, main, vt)


TIE_CHUNK = 256


KV_STEP = 256


def _dsa_kernel(ikk_ref, kk_ref, vvt_ref, iq_a, iwt_a, q_a, iq_b, iwt_b, q_b, o_ref, u_a, bias_a, u_b, bias_b):
    j = pl.program_id(1)
    n_qt = SEQ // TQ_DSA
    tiles_per_step = KV_STEP // TQ_DSA
    n_var = SEQ // KV_STEP
    for c in range(n_var // 2):
        @pl.when(j // tiles_per_step == c)
        def _(c=c):
            tiles = (((c + 1) * KV_STEP, j, iq_a, iwt_a, q_a, u_a, bias_a),
                     ((n_var - c) * KV_STEP, n_qt - 1 - j, iq_b, iwt_b, q_b, u_b, bias_b))
            _dsa_step(tiles, ikk_ref, kk_ref, vvt_ref, o_ref)


def _dsa_step(tiles, ikk_ref, kk_ref, vvt_ref, o_ref):
    searched = []
    for nk, qt, iq_ref, iwt_ref, _, u_ref, bias_ref in tiles:
        key_pos = lax.broadcasted_iota(jnp.int32, (nk, TQ_DSA), 0)
        q_pos = qt * TQ_DSA + lax.broadcasted_iota(jnp.int32, (nk, TQ_DSA), 1)
        causal = key_pos <= q_pos
        if nk <= B_TOPK:
            bias_ref[0:nk, :] = jnp.where(causal, 0.0, NEG_INF)
        else:
            _dsa_scores(nk, causal, ikk_ref, iq_ref, iwt_ref, u_ref)
            searched.append((nk, u_ref, bias_ref))
    found = _dsa_thresholds([(nk, u_ref) for nk, u_ref, _ in searched])
    for (nk, u_ref, bias_ref), (tau, n_ge) in zip(searched, found):
        _dsa_mask(nk, tau, n_ge, u_ref, bias_ref)
    for nk, qt, _, _, q_ref, _, bias_ref in tiles:
        _dsa_attend(nk, qt, kk_ref, vvt_ref, q_ref, bias_ref, o_ref)


def _dsa_attend(nk, qt, kk_ref, vvt_ref, q_ref, bias_ref, o_ref):
    q_all = _stack_heads(q_ref[...], B_HEADS)
    bias = bias_ref[0:nk, :]
    s = _dot_nt(kk_ref[0:nk, :], q_all) + jnp.concatenate([bias] * B_HEADS, axis=1)
    m = _colmax(s)
    p = jnp.exp2(s - m)
    l = _colsum(p)
    o = _dot(vvt_ref[:, 0:nk], p.astype(jnp.bfloat16)) / l
    o = jnp.concatenate([o[(h % 2) * HEAD_DIM:(h % 2 + 1) * HEAD_DIM, h * TQ_DSA:(h + 1) * TQ_DSA]
                         for h in range(B_HEADS)], axis=0)
    o_ref[pl.ds(pl.multiple_of(qt * TQ_DSA, TQ_DSA), TQ_DSA), :] = o.T.astype(o_ref.dtype)


def _dsa_scores(nk, causal, ikk_ref, iq_ref, iwt_ref, u_ref):
    iw = iwt_ref[...]
    iq_all = _stack_heads(iq_ref[...], B_IDX_HEADS)
    iw_all = jnp.concatenate([iw[h:h + 1, :] for h in range(B_IDX_HEADS)], axis=1)
    part = jnp.maximum(_dot_nt(ikk_ref[0:nk, :], iq_all), 0.0) * iw_all
    score = part[:, 0:TQ_DSA]
    for h in range(1, B_IDX_HEADS):
        score = score + part[:, h * TQ_DSA:(h + 1) * TQ_DSA]

    score = jnp.where(score == 0.0, 0.0, score)
    bits = pltpu.bitcast(score, jnp.int32)
    u = jnp.where(bits < 0, bits ^ jnp.int32(0x7FFFFFFF), bits)
    u_ref[0:nk, :] = jnp.where(causal, u, jnp.int32(INT_MIN))


def _dsa_thresholds(jobs):
    n_keep = float(B_TOPK)

    def bisect(i, carry):
        bit = jnp.left_shift(jnp.int32(1), 31 - i)
        out = []
        for (nk, u_ref), (tau, n_ge) in zip(jobs, carry):
            trial = tau ^ bit
            acc = jnp.zeros((LANES, TQ_DSA), jnp.float32)
            for c in range(nk // LANES):
                acc = jnp.where(u_ref[c * LANES:(c + 1) * LANES, :] >= trial, acc + 1.0, acc)
            cnt = acc.sum(axis=0, keepdims=True)
            ok = cnt >= n_keep
            out.append((jnp.where(ok, trial, tau), jnp.where(ok, cnt, n_ge)))
        return tuple(out)

    init = tuple((jnp.full((1, TQ_DSA), INT_MIN, jnp.int32), jnp.full((1, TQ_DSA), float(nk), jnp.float32))
                 for nk, _ in jobs)
    return lax.fori_loop(0, 32, bisect, init)


def _dsa_mask(nk, tau, n_ge, u_ref, bias_ref):
    far = 1e9
    excess = jnp.where(tau > jnp.int32(INT_MIN), n_ge - float(B_TOPK), far)
    r_i = lax.broadcasted_iota(jnp.int32, (TIE_CHUNK, TIE_CHUNK), 0)
    c_i = lax.broadcasted_iota(jnp.int32, (TIE_CHUNK, TIE_CHUNK), 1)
    above = jnp.where(c_i > r_i, 1.0, 0.0).astype(jnp.bfloat16)
    later = jnp.zeros((1, TQ_DSA), jnp.float32)
    for c in reversed(range(nk // TIE_CHUNK)):
        uc = u_ref[c * TIE_CHUNK:(c + 1) * TIE_CHUNK, :]
        eq = uc == tau
        eq_f = jnp.where(eq, 1.0, 0.0)
        after = _dot(above, eq_f.astype(jnp.bfloat16)) + later
        later = later + _colsum(eq_f)
        slack = jnp.where(uc > tau, far, jnp.where(eq, after, -1.0))
        bias_ref[c * TIE_CHUNK:(c + 1) * TIE_CHUNK, :] = jnp.where(slack >= excess, 0.0, NEG_INF)


def _stack_heads(x, n_heads):
    rows = []
    for h in range(n_heads):
        blk = x[:, (h // 2) * LANES:(h // 2 + 1) * LANES]
        rows.append(jnp.where(_head_mask(h % 2), blk, jnp.zeros_like(blk)))
    return jnp.concatenate(rows, axis=0)


def _dsa(main, vt, iwt, bsz):
    t = main.shape[0]
    n_qt = SEQ // TQ_DSA

    def tile_specs(tile):
        return [
            pl.BlockSpec((TQ_DSA, 2 * LANES),
                         lambda b, i: (b * n_qt + tile(i), _MAIN_OFF["i_q"] // (2 * LANES))),
            pl.BlockSpec((IWT_ROWS, TQ_DSA), lambda b, i: (0, b * n_qt + tile(i))),
            pl.BlockSpec((TQ_DSA, B_WIDTH), lambda b, i: (b * n_qt + tile(i), _MAIN_OFF["b_q"] // B_WIDTH)),
        ]

    return pl.pallas_call(
        _dsa_kernel,
        grid=(bsz, n_qt // 2),
        in_specs=[
            pl.BlockSpec((SEQ, LANES), lambda b, i: (b, _MAIN_OFF["i_kk"] // LANES)),
            pl.BlockSpec((SEQ, LANES), lambda b, i: (b, _MAIN_OFF["b_kk"] // LANES)),
            pl.BlockSpec((LANES, SEQ), lambda b, i: (A_WIDTH // LANES, b)),
        ] + tile_specs(lambda i: i) + tile_specs(lambda i: n_qt - 1 - i),
        out_specs=pl.BlockSpec((SEQ, B_WIDTH), lambda b, i: (b, 0)),
        out_shape=jax.ShapeDtypeStruct((t, B_WIDTH), jnp.bfloat16),
        scratch_shapes=[
            pltpu.VMEM((SEQ, TQ_DSA), jnp.int32),
            pltpu.VMEM((SEQ, TQ_DSA), jnp.float32),
        ] * 2,
        compiler_params=pltpu.CompilerParams(
            dimension_semantics=("parallel", "arbitrary"), vmem_limit_bytes=VMEM_LIMIT),
        name="dsa",
    )(main, main, vt, main, iwt, main, main, iwt, main)


N_CBLK = SEQ // C_BLOCK
N_GROUPS = len(C_GROUPS)


def _dilated_kernel(*refs):
    q_in = refs[0:N_GROUPS]
    k_in = refs[N_GROUPS:2 * N_GROUPS]
    v_in = refs[2 * N_GROUPS:3 * N_GROUPS]
    o_ref = refs[3 * N_GROUPS]
    scratch = refs[3 * N_GROUPS + 1:]
    tmp_ref = scratch[0]
    dense = scratch[1:1 + 3 * (N_GROUPS - 1)]
    od_ref, ld_ref = scratch[-2 - 2 * N_GROUPS], scratch[-1 - 2 * N_GROUPS]
    o_tok = scratch[-2 * N_GROUPS:-N_GROUPS]
    l_tok = scratch[-N_GROUPS:]

    lane = lax.broadcasted_iota(jnp.int32, (1, LANES), 1)
    low = lane < HEAD_DIM
    qi = lax.broadcasted_iota(jnp.int32, (C_BLOCK, C_BLOCK), 0)
    ki = lax.broadcasted_iota(jnp.int32, (C_BLOCK, C_BLOCK), 1)
    own_ok = ki <= qi

    for g, (window, dil) in enumerate(C_GROUPS):
        assert window // dil == C_BLOCK
        n_sub = SEQ // dil
        n_blk = n_sub // C_BLOCK
        if dil == 1:
            qd, kd, vd = q_in[g], k_in[g], v_in[g]
        else:
            qd, kd, vd = dense[3 * (g - 1):3 * g]
            for src, dst in ((q_in[g], qd), (k_in[g], kd), (v_in[g], vd)):
                tmp_ref[...] = src[...].astype(jnp.float32)
                for r in range(dil):
                    dst[r * n_sub:(r + 1) * n_sub, :] = tmp_ref[pl.ds(r, n_sub, stride=dil), :].astype(dst.dtype)
        o_dst = o_tok[g] if dil == 1 else od_ref
        l_dst = l_tok[g] if dil == 1 else ld_ref

        blocked = (N_CBLK, C_BLOCK, LANES)
        q3 = qd[...].reshape(blocked)
        k3 = kd[...].reshape(blocked)
        v3 = vd[...].reshape(blocked)
        if n_blk > 1:
            pad = jnp.zeros((1, C_BLOCK, LANES), k3.dtype)
            k3 = jnp.concatenate([jnp.concatenate([pad, k3[:-1]], axis=0), k3], axis=1)
            v3 = jnp.concatenate([jnp.concatenate([pad, v3[:-1]], axis=0), v3], axis=1)
            shape = (N_CBLK, C_BLOCK, 2 * C_BLOCK)
            b_i = lax.broadcasted_iota(jnp.int32, shape, 0)
            q_i = lax.broadcasted_iota(jnp.int32, shape, 1)
            k_i = lax.broadcasted_iota(jnp.int32, shape, 2)
            lo = jnp.minimum(q_i + jnp.where((b_i & (n_blk - 1)) == 0, C_BLOCK, 0), C_BLOCK)
            bias = jnp.where(k_i >= lo, jnp.where(k_i <= q_i + C_BLOCK, 0.0, NEG_INF), NEG_INF)
        else:
            bias = jnp.where(own_ok, 0.0, NEG_INF)[None]
        o_h, l_h = [], []
        for hh in range(2):
            qh = jnp.where(_head_mask(hh)[None], q3, jnp.zeros_like(q3))
            s = jnp.einsum("bqd,bkd->bqk", qh, k3, preferred_element_type=jnp.float32) + bias
            m = jnp.max(s, axis=-1, keepdims=True)
            p = jnp.exp2(s - m)
            l = jnp.sum(p, axis=-1, keepdims=True)
            o = jnp.einsum("bqk,bkd->bqd", p.astype(jnp.bfloat16), v3, preferred_element_type=jnp.float32)
            o_h.append(o / l)
            l_h.append(m + jnp.log2(l))
        o_dst[...] = jnp.where(low[None], o_h[0], o_h[1]).reshape(SEQ, LANES)
        l_dst[...] = jnp.where(low[None], l_h[0], l_h[1]).reshape(SEQ, LANES)
        if dil > 1:
            for r in range(dil):
                o_tok[g][pl.ds(r, n_sub, stride=dil), :] = od_ref[r * n_sub:(r + 1) * n_sub, :]
                l_tok[g][pl.ds(r, n_sub, stride=dil), :] = ld_ref[r * n_sub:(r + 1) * n_sub, :]

    lses = [l_tok[g][...] for g in range(N_GROUPS)]
    top = functools.reduce(jnp.maximum, lses)
    es = [jnp.exp2(x - top) for x in lses]
    num = sum(e * o_tok[g][...] for g, e in enumerate(es))
    o_ref[...] = (num / sum(es)).astype(o_ref.dtype)


def _dilated(main, bsz):
    t = main.shape[0]
    n_pairs = C_SLOTS // 2

    def col_spec(name, g):
        base = _MAIN_OFF[name] // LANES + g * n_pairs
        return pl.BlockSpec((SEQ, LANES), lambda b, p: (b, base + p))

    in_specs = ([col_spec("c_q", g) for g in range(N_GROUPS)]
                + [col_spec("c_k", g) for g in range(N_GROUPS)]
                + [col_spec("c_v", g) for g in range(N_GROUPS)])
    scratch = ([pltpu.VMEM((SEQ, LANES), jnp.float32)]
               + [pltpu.VMEM((SEQ, LANES), jnp.bfloat16)] * (3 * (N_GROUPS - 1))
               + [pltpu.VMEM((SEQ, LANES), jnp.float32)] * (2 + 2 * N_GROUPS))
    return pl.pallas_call(
        _dilated_kernel,
        grid=(bsz, n_pairs),
        in_specs=in_specs,
        out_specs=pl.BlockSpec((SEQ, LANES), lambda b, p: (b, p)),
        out_shape=jax.ShapeDtypeStruct((t, C_WIDTH), jnp.bfloat16),
        scratch_shapes=scratch,
        compiler_params=pltpu.CompilerParams(
            dimension_semantics=("parallel", "parallel"), vmem_limit_bytes=VMEM_LIMIT),
        name="dilated",
    )(*([main] * (3 * N_GROUPS)))


def _sigmoid(x):
    return 1.0 / (1.0 + jnp.exp(-x))


def _merge_kernel(x_ref, g_ref, ya_ref, yb_ref, yc_ref, wg_ref, wa_ref, wb_ref, wc_ref, wo_ref, fg_ref,
                  o_ref, *, final):
    x = x_ref[...]
    h = _rms(x, g_ref[...]).astype(jnp.bfloat16)
    merged = jnp.zeros(x.shape, jnp.float32)
    c0 = 0
    m0 = A_WIDTH + B_WIDTH + C_WIDTH
    for i, (y_ref, w_ref) in enumerate(((ya_ref, wa_ref), (yb_ref, wb_ref), (yc_ref, wc_ref))):
        width = y_ref.shape[1]
        gate = _dot(h, wg_ref[:, c0:c0 + width])
        z = (y_ref[...].astype(jnp.float32) * (gate * _sigmoid(gate))).astype(jnp.bfloat16)
        mix = _sigmoid(_dot(h, wg_ref[:, m0 + i * D_MODEL:m0 + (i + 1) * D_MODEL]))
        merged = merged + mix * _dot(z, w_ref[...])
        c0 += width
    out = x + _dot(merged.astype(jnp.bfloat16), wo_ref[...])
    if final:
        out = _rms(out, fg_ref[...])
    o_ref[...] = out


def _merge(x2, g, ya, yb, yc, w_gate, w_a, w_b, w_c, w_o, fg, final):
    t = x2.shape[0]
    row = lambda w: pl.BlockSpec((TM_MERGE, w), lambda i: (i, 0))
    full = lambda a: pl.BlockSpec(a.shape, lambda i: (0, 0), pipeline_mode=pl.Buffered(1))
    return pl.pallas_call(
        functools.partial(_merge_kernel, final=final),
        grid=(t // TM_MERGE,),
        in_specs=[row(D_MODEL), full(g), row(A_WIDTH), row(B_WIDTH), row(C_WIDTH),
                  full(w_gate), full(w_a), full(w_b), full(w_c), full(w_o), full(fg)],
        out_specs=row(D_MODEL),
        out_shape=jax.ShapeDtypeStruct((t, D_MODEL), jnp.float32),
        compiler_params=pltpu.CompilerParams(dimension_semantics=("parallel",),
                                             vmem_limit_bytes=VMEM_LIMIT),
        name="merge_final" if final else "merge",
    )(x2, g, ya, yb, yc, w_gate, w_a, w_b, w_c, w_o, fg)


def _rope_tables():
    inv_freq = 1.0 / (ROPE_THETA ** (jnp.arange(0, HEAD_DIM, 2, dtype=jnp.float32) / HEAD_DIM))
    ang = jnp.arange(SEQ, dtype=jnp.float32)[:, None] * inv_freq[None, :]
    cos, sin = jnp.cos(ang), jnp.sin(ang)
    reps = LANES // HEAD_DIM
    cos_t = jnp.tile(jnp.concatenate([cos, cos], axis=1), (1, reps))
    sin_t = jnp.tile(jnp.concatenate([-sin, sin], axis=1), (1, reps))
    return cos_t, sin_t


def _split_w_in(w):
    names = ("a_q", "a_k", "a_v", "a_g", "b_q", "b_k", "b_v", "b_g", "i_q", "i_k", "i_w",
             "c_q", "c_k", "c_v", "c_g", "m_g")
    return {n: w[:, _OFF[i]:_OFF[i + 1]] for i, n in enumerate(names)}


def _column_scales():
    scale = np.ones((_OFF[-1],), np.float32)
    names = {"a_q": 0, "b_q": 4, "c_q": 11}
    for i in names.values():
        scale[_OFF[i]:_OFF[i + 1]] = HEAD_DIM ** -0.5 * LOG2_E
    scale[_OFF[10]:_OFF[11]] = (B_IDX_HEADS * B_IDX_DIM) ** -0.5
    return scale


def _layer_weights(w_bf):
    p = _split_w_in(w_bf)
    main = jnp.concatenate([
        p["a_q"], p["a_k"], p["b_q"], p["b_k"], p["b_k"],
        p["i_q"], p["i_k"], p["i_k"], p["c_q"], p["c_k"], p["c_v"]], axis=1)
    w_t = jnp.concatenate([
        p["a_v"].T, p["b_v"].T, p["b_v"].T, p["i_w"].T,
        jnp.zeros((IWT_ROWS - B_IDX_HEADS, D_MODEL), w_bf.dtype)], axis=0)
    gates = jnp.concatenate([p["a_g"], p["b_g"], p["c_g"], p["m_g"]], axis=1)
    return main, w_t, gates


def kernel(x, norm_g, w_in, w_br_a, w_br_b, w_br_c, w_out, final_norm_g):
    bsz, seq, d_model = x.shape
    assert seq == SEQ and d_model == D_MODEL
    assert MAIN_WIDTH % LANES == 0 and GATE_WIDTH == sum(IN_SPLITS[i] for i in (3, 7, 14, 15))
    cos_t, sin_t = _rope_tables()
    bf = jnp.bfloat16
    x2 = x.reshape(bsz * seq, d_model)
    fg = final_norm_g.reshape(1, d_model)
    w_in_bf = (w_in * _column_scales()).astype(bf)
    for layer in range(DEPTH):
        w_main, w_t, w_gate = _layer_weights(w_in_bf[layer])
        g = norm_g[layer].reshape(1, d_model)
        main, vt, iwt = _proj(x2, g, w_main, w_t, cos_t, sin_t)
        ya = _moba(main, vt, bsz)
        yb = _dsa(main, vt, iwt, bsz)
        yc = _dilated(main, bsz)
        x2 = _merge(x2, g, ya, yb, yc, w_gate, w_br_a[layer].astype(bf), w_br_b[layer].astype(bf),
                    w_br_c[layer].astype(bf), w_out[layer].astype(bf), fg, final=(layer == DEPTH - 1))
    return x2.reshape(bsz, seq, d_model)
```

```python
import functools

import numpy as np
import jax
import jax.numpy as jnp
from jax import lax
from jax.experimental import pallas as pl
from jax.experimental.pallas import tpu as pltpu

D_MODEL = 1024
SEQ = 2048
DEPTH = 2
HEAD_DIM = 64
ROPE_THETA = 10000.0
RMS_EPS = 1e-6
N_BRANCH = 3

A_HEADS = 6
A_WIDTH = A_HEADS * HEAD_DIM
A_BLOCK = 256
A_TOPK = 3

B_HEADS = 6
B_WIDTH = B_HEADS * HEAD_DIM
B_TOPK = 256
B_IDX_HEADS = 4
B_IDX_DIM = HEAD_DIM

C_GROUPS = ((128, 1), (512, 4), (2048, 16))
C_SLOTS = 4
C_HEADS = C_SLOTS * len(C_GROUPS)
C_WIDTH = C_SLOTS * HEAD_DIM
C_BLOCK = 128

IN_SPLITS = (A_WIDTH, A_WIDTH, A_WIDTH, A_WIDTH,
             B_WIDTH, HEAD_DIM, HEAD_DIM, B_WIDTH, B_IDX_HEADS * B_IDX_DIM, B_IDX_DIM, B_IDX_HEADS,
             C_HEADS * HEAD_DIM, C_HEADS * HEAD_DIM, C_HEADS * HEAD_DIM, C_WIDTH,
             N_BRANCH * D_MODEL)
_OFF = tuple(int(o) for o in np.cumsum((0,) + IN_SPLITS))

LANES = 128
VMEM_LIMIT = 56 * 1024 * 1024

_MAIN_GROUPS = (("a_q", A_WIDTH), ("a_k", A_WIDTH), ("b_q", B_WIDTH), ("b_kk", LANES),
                ("i_q", B_IDX_HEADS * B_IDX_DIM), ("i_kk", LANES),
                ("c_q", C_HEADS * HEAD_DIM), ("c_k", C_HEADS * HEAD_DIM), ("c_v", C_HEADS * HEAD_DIM))
_MAIN_OFF = {}
_c = 0
for _n, _w in _MAIN_GROUPS:
    _MAIN_OFF[_n] = _c
    _c += _w
MAIN_WIDTH = _c
ROPE_WIDTH = _MAIN_OFF["c_v"]
VT_ROWS = A_WIDTH + LANES
IWT_ROWS = 8
GATE_WIDTH = A_WIDTH + B_WIDTH + C_WIDTH + N_BRANCH * D_MODEL

TM_PROJ = 1024
TM_MERGE = 1024
TQ_DSA = 128
INT_MIN = -2 ** 31
LOG2_E = 1.4426950408889634
NEG_INF = float("-inf")


def _dot(a, b):
    return jnp.dot(a, b, preferred_element_type=jnp.float32)


def _dot_nt(a, b):
    return lax.dot_general(a, b, (((1,), (1,)), ((), ())), preferred_element_type=jnp.float32)


def _colsum(x):
    n = x.shape[0]
    part = x.reshape(n // LANES, LANES, x.shape[1]).sum(axis=0)
    return part.sum(axis=0, keepdims=True)


def _colmax(x):
    n = x.shape[0]
    part = x.reshape(n // LANES, LANES, x.shape[1]).max(axis=0)
    return part.max(axis=0, keepdims=True)


def _rms(x, g):
    return x * lax.rsqrt(jnp.mean(x * x, axis=-1, keepdims=True) + RMS_EPS) * g


def _head_mask(hh):
    lane = lax.broadcasted_iota(jnp.int32, (1, LANES), 1)
    return (lane >= hh * HEAD_DIM) & (lane < (hh + 1) * HEAD_DIM)


def _proj_kernel(x_ref, g_ref, w_ref, wt_ref, cos_ref, sin_ref, main_ref, vt_ref, iwt_ref):
    h = _rms(x_ref[...], g_ref[...]).astype(jnp.bfloat16)
    cos = cos_ref[...]
    sin = sin_ref[...]
    lane = lax.broadcasted_iota(jnp.int32, (1, LANES), 1)
    first_half = (lane % HEAD_DIM) < (HEAD_DIM // 2)
    chunk = 4 * LANES
    for c0 in range(0, MAIN_WIDTH, chunk):
        w = min(chunk, MAIN_WIDTH - c0)
        res = _dot(h, w_ref[:, c0:c0 + w])
        for j in range(0, w, LANES):
            blk = res[:, j:j + LANES]
            if c0 + j < ROPE_WIDTH:
                partner = jnp.where(first_half, pltpu.roll(blk, LANES - HEAD_DIM // 2, 1),
                                    pltpu.roll(blk, HEAD_DIM // 2, 1))
                blk = blk * cos + partner * sin
            main_ref[:, c0 + j:c0 + j + LANES] = blk.astype(main_ref.dtype)
    rt = _dot_nt(wt_ref[...], h)
    vt_ref[...] = rt[:VT_ROWS].astype(vt_ref.dtype)
    iwt_ref[...] = rt[VT_ROWS:]


def _proj(x2, g, w_main, w_t, cos_t, sin_t):
    t = x2.shape[0]
    n_rope_tiles = SEQ // TM_PROJ
    return pl.pallas_call(
        _proj_kernel,
        grid=(t // TM_PROJ,),
        in_specs=[
            pl.BlockSpec((TM_PROJ, D_MODEL), lambda i: (i, 0)),
            pl.BlockSpec((1, D_MODEL), lambda i: (0, 0), pipeline_mode=pl.Buffered(1)),
            pl.BlockSpec((D_MODEL, MAIN_WIDTH), lambda i: (0, 0), pipeline_mode=pl.Buffered(1)),
            pl.BlockSpec((VT_ROWS + IWT_ROWS, D_MODEL), lambda i: (0, 0), pipeline_mode=pl.Buffered(1)),
            pl.BlockSpec((TM_PROJ, LANES), lambda i: (i % n_rope_tiles, 0)),
            pl.BlockSpec((TM_PROJ, LANES), lambda i: (i % n_rope_tiles, 0)),
        ],
        out_specs=[
            pl.BlockSpec((TM_PROJ, MAIN_WIDTH), lambda i: (i, 0)),
            pl.BlockSpec((VT_ROWS, TM_PROJ), lambda i: (0, i)),
            pl.BlockSpec((IWT_ROWS, TM_PROJ), lambda i: (0, i)),
        ],
        out_shape=[
            jax.ShapeDtypeStruct((t, MAIN_WIDTH), jnp.bfloat16),
            jax.ShapeDtypeStruct((VT_ROWS, t), jnp.bfloat16),
            jax.ShapeDtypeStruct((IWT_ROWS, t), jnp.float32),
        ],
        compiler_params=pltpu.CompilerParams(dimension_semantics=("parallel",),
                                             vmem_limit_bytes=VMEM_LIMIT),
        name="proj",
    )(x2, g, w_main, w_t, cos_t, sin_t)


N_ABLK = SEQ // A_BLOCK


def _moba_kernel(q_ref, k_ref, vt_ref, o_ref):
    kmean = jnp.concatenate(
        [jnp.mean(k_ref[n * A_BLOCK:(n + 1) * A_BLOCK, :].astype(jnp.float32), axis=0, keepdims=True)
         for n in range(N_ABLK)], axis=0).astype(jnp.bfloat16)
    for own in range(N_ABLK):
        _moba_body(own, kmean, q_ref, k_ref, vt_ref, o_ref)


def _moba_body(own, kmean, q_ref, k_ref, vt_ref, o_ref):
    nk = (own + 1) * A_BLOCK
    q = q_ref[own * A_BLOCK:nk, :]
    cols = 2 * A_BLOCK
    blk_f = lax.broadcasted_iota(jnp.int32, (N_ABLK, cols), 0).astype(jnp.float32)
    kpos = lax.broadcasted_iota(jnp.int32, (A_BLOCK, cols), 0)
    qpos = lax.broadcasted_iota(jnp.int32, (A_BLOCK, cols), 1) & (A_BLOCK - 1)
    causal = kpos <= qpos
    q2 = jnp.concatenate([jnp.where(_head_mask(hh), q, jnp.zeros_like(q)) for hh in range(2)], axis=0)
    gate = _dot_nt(kmean, q2)
    gate = jnp.where(blk_f < float(own), gate, NEG_INF)
    sel = jnp.zeros(gate.shape, jnp.float32)
    for _ in range(min(A_TOPK, own)):
        top = jnp.max(gate, axis=0, keepdims=True)
        is_top = (gate == top) & (top > NEG_INF)
        idx = jnp.min(jnp.where(is_top, blk_f, float(N_ABLK)), axis=0, keepdims=True)
        pick = blk_f == idx
        sel = jnp.where(pick, 1.0, sel)
        gate = jnp.where(pick, NEG_INF, gate)
    s = _dot_nt(k_ref[0:nk, :], q2)
    parts = [jnp.where(sel[n:n + 1, :] > 0.0, s[n * A_BLOCK:(n + 1) * A_BLOCK], NEG_INF)
             for n in range(own)]
    parts.append(jnp.where(causal, s[own * A_BLOCK:], NEG_INF))
    s = jnp.concatenate(parts, axis=0)
    m = _colmax(s)
    p = jnp.exp2(s - m)
    l = _colsum(p)
    o = _dot(vt_ref[:, 0:nk], p.astype(jnp.bfloat16)) / l
    o = jnp.concatenate([o[hh * HEAD_DIM:(hh + 1) * HEAD_DIM, hh * A_BLOCK:(hh + 1) * A_BLOCK]
                         for hh in range(2)], axis=0)
    o_ref[own * A_BLOCK:nk, :] = o.T.astype(o_ref.dtype)


def _moba(main, vt, bsz):
    t = main.shape[0]
    n_pairs = A_HEADS // 2
    q_col = _MAIN_OFF["a_q"] // LANES
    k_col = _MAIN_OFF["a_k"] // LANES
    return pl.pallas_call(
        _moba_kernel,
        grid=(bsz, n_pairs),
        in_specs=[
            pl.BlockSpec((SEQ, LANES), lambda b, p: (b, q_col + p)),
            pl.BlockSpec((SEQ, LANES), lambda b, p: (b, k_col + p)),
            pl.BlockSpec((LANES, SEQ), lambda b, p: (p, b)),
        ],
        out_specs=pl.BlockSpec((SEQ, LANES), lambda b, p: (b, p)),
        out_shape=jax.ShapeDtypeStruct((t, A_WIDTH), jnp.bfloat16),
        compiler_params=pltpu.CompilerParams(
            dimension_semantics=("parallel", "parallel"), vmem_limit_bytes=VMEM_LIMIT),
        name="moba",
    )(main, main, vt)


TIE_CHUNK = 256


KV_STEP = 256


def _dsa_kernel(ikk_ref, kk_ref, vvt_ref, iq_ref, iwt_ref, q_ref, o_ref, *scratch):
    half = pl.program_id(1)
    n_qt = SEQ // TQ_DSA
    tiles_per_step = KV_STEP // TQ_DSA
    n_var = SEQ // KV_STEP
    for c in range(n_var // 2):
        j = tiles_per_step * c + half
        u_a, bias_a, u_b, bias_b = scratch[4 * c:4 * c + 4]
        tiles = (((c + 1) * KV_STEP, j, u_a, bias_a), ((n_var - c) * KV_STEP, n_qt - 1 - j, u_b, bias_b))
        _dsa_step(tiles, ikk_ref, kk_ref, vvt_ref, iq_ref, iwt_ref, q_ref, o_ref)


def _tile_rows(qt):
    return pl.ds(pl.multiple_of(qt * TQ_DSA, TQ_DSA), TQ_DSA)


def _dsa_step(tiles, ikk_ref, kk_ref, vvt_ref, iq_ref, iwt_ref, q_ref, o_ref):
    searched = []
    for nk, qt, u_ref, bias_ref in tiles:
        key_pos = lax.broadcasted_iota(jnp.int32, (nk, TQ_DSA), 0)
        q_pos = qt * TQ_DSA + lax.broadcasted_iota(jnp.int32, (nk, TQ_DSA), 1)
        causal = key_pos <= q_pos
        if nk <= B_TOPK:
            bias_ref[0:nk, :] = jnp.where(causal, 0.0, NEG_INF)
        else:
            _dsa_scores(nk, causal, ikk_ref, iq_ref[_tile_rows(qt), :], iwt_ref[qt], u_ref)
            searched.append((nk, u_ref, bias_ref))
    found = _dsa_thresholds([(nk, u_ref) for nk, u_ref, _ in searched])
    for (nk, u_ref, bias_ref), (tau, n_ge) in zip(searched, found):
        _dsa_mask(nk, tau, n_ge, u_ref, bias_ref)
    for nk, qt, _, bias_ref in tiles:
        _dsa_attend(nk, qt, kk_ref, vvt_ref, q_ref[_tile_rows(qt), :], bias_ref, o_ref)


def _dsa_attend(nk, qt, kk_ref, vvt_ref, q, bias_ref, o_ref):
    q_all = _stack_heads(q, B_HEADS)
    bias = bias_ref[0:nk, :]
    s = _dot_nt(kk_ref[0:nk, :], q_all) + jnp.concatenate([bias] * B_HEADS, axis=1)
    m = _colmax(s)
    p = jnp.exp2(s - m)
    l = _colsum(p)
    o = _dot(vvt_ref[:, 0:nk], p.astype(jnp.bfloat16)) / l
    o = jnp.concatenate([o[(h % 2) * HEAD_DIM:(h % 2 + 1) * HEAD_DIM, h * TQ_DSA:(h + 1) * TQ_DSA]
                         for h in range(B_HEADS)], axis=0)
    o_ref[_tile_rows(qt), :] = o.T.astype(o_ref.dtype)


def _dsa_scores(nk, causal, ikk_ref, iq, iw, u_ref):
    iq_all = _stack_heads(iq, B_IDX_HEADS)
    iw_all = jnp.concatenate([iw[h:h + 1, :] for h in range(B_IDX_HEADS)], axis=1)
    part = jnp.maximum(_dot_nt(ikk_ref[0:nk, :], iq_all), 0.0) * iw_all
    score = part[:, 0:TQ_DSA]
    for h in range(1, B_IDX_HEADS):
        score = score + part[:, h * TQ_DSA:(h + 1) * TQ_DSA]

    score = jnp.where(score == 0.0, 0.0, score)
    bits = pltpu.bitcast(score, jnp.int32)
    u = jnp.where(bits < 0, bits ^ jnp.int32(0x7FFFFFFF), bits)
    u_ref[0:nk, :] = jnp.where(causal, u, jnp.int32(INT_MIN))


def _dsa_thresholds(jobs):
    n_keep = float(B_TOPK)

    def bisect(i, carry):
        bit = jnp.left_shift(jnp.int32(1), 31 - i)
        out = []
        for (nk, u_ref), (tau, n_ge) in zip(jobs, carry):
            trial = tau ^ bit
            acc = jnp.zeros((LANES, TQ_DSA), jnp.float32)
            for c in range(nk // LANES):
                acc = jnp.where(u_ref[c * LANES:(c + 1) * LANES, :] >= trial, acc + 1.0, acc)
            cnt = acc.sum(axis=0, keepdims=True)
            ok = cnt >= n_keep
            out.append((jnp.where(ok, trial, tau), jnp.where(ok, cnt, n_ge)))
        return tuple(out)

    init = tuple((jnp.full((1, TQ_DSA), INT_MIN, jnp.int32), jnp.full((1, TQ_DSA), float(nk), jnp.float32))
                 for nk, _ in jobs)
    return lax.fori_loop(0, 32, bisect, init)


def _dsa_mask(nk, tau, n_ge, u_ref, bias_ref):
    far = 1e9
    excess = jnp.where(tau > jnp.int32(INT_MIN), n_ge - float(B_TOPK), far)
    r_i = lax.broadcasted_iota(jnp.int32, (TIE_CHUNK, TIE_CHUNK), 0)
    c_i = lax.broadcasted_iota(jnp.int32, (TIE_CHUNK, TIE_CHUNK), 1)
    above = jnp.where(c_i > r_i, 1.0, 0.0).astype(jnp.bfloat16)
    later = jnp.zeros((1, TQ_DSA), jnp.float32)
    for c in reversed(range(nk // TIE_CHUNK)):
        uc = u_ref[c * TIE_CHUNK:(c + 1) * TIE_CHUNK, :]
        eq = uc == tau
        eq_f = jnp.where(eq, 1.0, 0.0)
        after = _dot(above, eq_f.astype(jnp.bfloat16)) + later
        later = later + _colsum(eq_f)
        slack = jnp.where(uc > tau, far, jnp.where(eq, after, -1.0))
        bias_ref[c * TIE_CHUNK:(c + 1) * TIE_CHUNK, :] = jnp.where(slack >= excess, 0.0, NEG_INF)


def _stack_heads(x, n_heads):
    rows = []
    for h in range(n_heads):
        blk = x[:, (h // 2) * LANES:(h // 2 + 1) * LANES]
        rows.append(jnp.where(_head_mask(h % 2), blk, jnp.zeros_like(blk)))
    return jnp.concatenate(rows, axis=0)


def _dsa(main, vt, iwt, bsz):
    t = main.shape[0]
    n_qt = SEQ // TQ_DSA
    n_var = SEQ // KV_STEP
    iwt_tiles = iwt.reshape(IWT_ROWS, t // TQ_DSA, TQ_DSA).transpose(1, 0, 2)
    scratch = []
    for c in range(n_var // 2):
        for nk in ((c + 1) * KV_STEP, (n_var - c) * KV_STEP):
            scratch += [pltpu.VMEM((nk, TQ_DSA), jnp.int32), pltpu.VMEM((nk, TQ_DSA), jnp.float32)]
    return pl.pallas_call(
        _dsa_kernel,
        grid=(bsz, KV_STEP // TQ_DSA),
        in_specs=[
            pl.BlockSpec((SEQ, LANES), lambda b, i: (b, _MAIN_OFF["i_kk"] // LANES)),
            pl.BlockSpec((SEQ, LANES), lambda b, i: (b, _MAIN_OFF["b_kk"] // LANES)),
            pl.BlockSpec((LANES, SEQ), lambda b, i: (A_WIDTH // LANES, b)),
            pl.BlockSpec((SEQ, 2 * LANES), lambda b, i: (b, _MAIN_OFF["i_q"] // (2 * LANES))),
            pl.BlockSpec((n_qt, IWT_ROWS, TQ_DSA), lambda b, i: (b, 0, 0)),
            pl.BlockSpec((SEQ, B_WIDTH), lambda b, i: (b, _MAIN_OFF["b_q"] // B_WIDTH)),
        ],
        out_specs=pl.BlockSpec((SEQ, B_WIDTH), lambda b, i: (b, 0)),
        out_shape=jax.ShapeDtypeStruct((t, B_WIDTH), jnp.bfloat16),
        scratch_shapes=scratch,
        compiler_params=pltpu.CompilerParams(
            dimension_semantics=("parallel", "arbitrary"), vmem_limit_bytes=VMEM_LIMIT),
        name="dsa",
    )(main, main, vt, main, iwt_tiles, main)


N_CBLK = SEQ // C_BLOCK
N_GROUPS = len(C_GROUPS)


def _dilated_kernel(*refs):
    q_in = refs[0:N_GROUPS]
    k_in = refs[N_GROUPS:2 * N_GROUPS]
    v_in = refs[2 * N_GROUPS:3 * N_GROUPS]
    o_ref = refs[3 * N_GROUPS]
    scratch = refs[3 * N_GROUPS + 1:]
    tmp_ref = scratch[0]
    dense = scratch[1:1 + 3 * (N_GROUPS - 1)]
    od_ref, ld_ref = scratch[-2 - 2 * N_GROUPS], scratch[-1 - 2 * N_GROUPS]
    o_tok = scratch[-2 * N_GROUPS:-N_GROUPS]
    l_tok = scratch[-N_GROUPS:]

    lane = lax.broadcasted_iota(jnp.int32, (1, LANES), 1)
    low = lane < HEAD_DIM
    qi = lax.broadcasted_iota(jnp.int32, (C_BLOCK, C_BLOCK), 0)
    ki = lax.broadcasted_iota(jnp.int32, (C_BLOCK, C_BLOCK), 1)
    own_ok = ki <= qi

    for g, (window, dil) in enumerate(C_GROUPS):
        assert window // dil == C_BLOCK
        n_sub = SEQ // dil
        n_blk = n_sub // C_BLOCK
        if dil == 1:
            qd, kd, vd = q_in[g], k_in[g], v_in[g]
        else:
            qd, kd, vd = dense[3 * (g - 1):3 * g]
            for src, dst in ((q_in[g], qd), (k_in[g], kd), (v_in[g], vd)):
                tmp_ref[...] = src[...].astype(jnp.float32)
                for r in range(dil):
                    dst[r * n_sub:(r + 1) * n_sub, :] = tmp_ref[pl.ds(r, n_sub, stride=dil), :].astype(dst.dtype)
        o_dst = o_tok[g] if dil == 1 else od_ref
        l_dst = l_tok[g] if dil == 1 else ld_ref

        blocked = (N_CBLK, C_BLOCK, LANES)
        q3 = qd[...].reshape(blocked)
        k3 = kd[...].reshape(blocked)
        v3 = vd[...].reshape(blocked)
        if n_blk > 1:
            pad = jnp.zeros((1, C_BLOCK, LANES), k3.dtype)
            k3 = jnp.concatenate([jnp.concatenate([pad, k3[:-1]], axis=0), k3], axis=1)
            v3 = jnp.concatenate([jnp.concatenate([pad, v3[:-1]], axis=0), v3], axis=1)
            shape = (N_CBLK, C_BLOCK, 2 * C_BLOCK)
            b_i = lax.broadcasted_iota(jnp.int32, shape, 0)
            q_i = lax.broadcasted_iota(jnp.int32, shape, 1)
            k_i = lax.broadcasted_iota(jnp.int32, shape, 2)
            lo = jnp.minimum(q_i + jnp.where((b_i & (n_blk - 1)) == 0, C_BLOCK, 0), C_BLOCK)
            bias = jnp.where(k_i >= lo, jnp.where(k_i <= q_i + C_BLOCK, 0.0, NEG_INF), NEG_INF)
        else:
            bias = jnp.where(own_ok, 0.0, NEG_INF)[None]
        o_h, l_h = [], []
        for hh in range(2):
            qh = jnp.where(_head_mask(hh)[None], q3, jnp.zeros_like(q3))
            s = jnp.einsum("bqd,bkd->bqk", qh, k3, preferred_element_type=jnp.float32) + bias
            m = jnp.max(s, axis=-1, keepdims=True)
            p = jnp.exp2(s - m)
            l = jnp.sum(p, axis=-1, keepdims=True)
            o = jnp.einsum("bqk,bkd->bqd", p.astype(jnp.bfloat16), v3, preferred_element_type=jnp.float32)
            o_h.append(o / l)
            l_h.append(m + jnp.log2(l))
        o_dst[...] = jnp.where(low[None], o_h[0], o_h[1]).reshape(SEQ, LANES)
        l_dst[...] = jnp.where(low[None], l_h[0], l_h[1]).reshape(SEQ, LANES)
        if dil > 1:
            for r in range(dil):
                o_tok[g][pl.ds(r, n_sub, stride=dil), :] = od_ref[r * n_sub:(r + 1) * n_sub, :]
                l_tok[g][pl.ds(r, n_sub, stride=dil), :] = ld_ref[r * n_sub:(r + 1) * n_sub, :]

    lses = [l_tok[g][...] for g in range(N_GROUPS)]
    top = functools.reduce(jnp.maximum, lses)
    es = [jnp.exp2(x - top) for x in lses]
    num = sum(e * o_tok[g][...] for g, e in enumerate(es))
    o_ref[...] = (num / sum(es)).astype(o_ref.dtype)


def _dilated(main, bsz):
    t = main.shape[0]
    n_pairs = C_SLOTS // 2

    def col_spec(name, g):
        base = _MAIN_OFF[name] // LANES + g * n_pairs
        return pl.BlockSpec((SEQ, LANES), lambda b, p: (b, base + p))

    in_specs = ([col_spec("c_q", g) for g in range(N_GROUPS)]
                + [col_spec("c_k", g) for g in range(N_GROUPS)]
                + [col_spec("c_v", g) for g in range(N_GROUPS)])
    scratch = ([pltpu.VMEM((SEQ, LANES), jnp.float32)]
               + [pltpu.VMEM((SEQ, LANES), jnp.bfloat16)] * (3 * (N_GROUPS - 1))
               + [pltpu.VMEM((SEQ, LANES), jnp.float32)] * (2 + 2 * N_GROUPS))
    return pl.pallas_call(
        _dilated_kernel,
        grid=(bsz, n_pairs),
        in_specs=in_specs,
        out_specs=pl.BlockSpec((SEQ, LANES), lambda b, p: (b, p)),
        out_shape=jax.ShapeDtypeStruct((t, C_WIDTH), jnp.bfloat16),
        scratch_shapes=scratch,
        compiler_params=pltpu.CompilerParams(
            dimension_semantics=("parallel", "parallel"), vmem_limit_bytes=VMEM_LIMIT),
        name="dilated",
    )(*([main] * (3 * N_GROUPS)))


def _sigmoid(x):
    return 1.0 / (1.0 + jnp.exp(-x))


def _merge_kernel(x_ref, g_ref, ya_ref, yb_ref, yc_ref, wg_ref, wa_ref, wb_ref, wc_ref, wo_ref, fg_ref,
                  o_ref, *, final):
    x = x_ref[...]
    h = _rms(x, g_ref[...]).astype(jnp.bfloat16)
    merged = jnp.zeros(x.shape, jnp.float32)
    c0 = 0
    m0 = A_WIDTH + B_WIDTH + C_WIDTH
    for i, (y_ref, w_ref) in enumerate(((ya_ref, wa_ref), (yb_ref, wb_ref), (yc_ref, wc_ref))):
        width = y_ref.shape[1]
        gate = _dot(h, wg_ref[:, c0:c0 + width])
        z = (y_ref[...].astype(jnp.float32) * (gate * _sigmoid(gate))).astype(jnp.bfloat16)
        mix = _sigmoid(_dot(h, wg_ref[:, m0 + i * D_MODEL:m0 + (i + 1) * D_MODEL]))
        merged = merged + mix * _dot(z, w_ref[...])
        c0 += width
    out = x + _dot(merged.astype(jnp.bfloat16), wo_ref[...])
    if final:
        out = _rms(out, fg_ref[...])
    o_ref[...] = out


def _merge(x2, g, ya, yb, yc, w_gate, w_a, w_b, w_c, w_o, fg, final):
    t = x2.shape[0]
    row = lambda w: pl.BlockSpec((TM_MERGE, w), lambda i: (i, 0))
    full = lambda a: pl.BlockSpec(a.shape, lambda i: (0, 0), pipeline_mode=pl.Buffered(1))
    return pl.pallas_call(
        functools.partial(_merge_kernel, final=final),
        grid=(t // TM_MERGE,),
        in_specs=[row(D_MODEL), full(g), row(A_WIDTH), row(B_WIDTH), row(C_WIDTH),
                  full(w_gate), full(w_a), full(w_b), full(w_c), full(w_o), full(fg)],
        out_specs=row(D_MODEL),
        out_shape=jax.ShapeDtypeStruct((t, D_MODEL), jnp.float32),
        compiler_params=pltpu.CompilerParams(dimension_semantics=("parallel",),
                                             vmem_limit_bytes=VMEM_LIMIT),
        name="merge_final" if final else "merge",
    )(x2, g, ya, yb, yc, w_gate, w_a, w_b, w_c, w_o, fg)


def _rope_tables():
    inv_freq = 1.0 / (ROPE_THETA ** (jnp.arange(0, HEAD_DIM, 2, dtype=jnp.float32) / HEAD_DIM))
    ang = jnp.arange(SEQ, dtype=jnp.float32)[:, None] * inv_freq[None, :]
    cos, sin = jnp.cos(ang), jnp.sin(ang)
    reps = LANES // HEAD_DIM
    cos_t = jnp.tile(jnp.concatenate([cos, cos], axis=1), (1, reps))
    sin_t = jnp.tile(jnp.concatenate([-sin, sin], axis=1), (1, reps))
    return cos_t, sin_t


def _split_w_in(w):
    names = ("a_q", "a_k", "a_v", "a_g", "b_q", "b_k", "b_v", "b_g", "i_q", "i_k", "i_w",
             "c_q", "c_k", "c_v", "c_g", "m_g")
    return {n: w[:, _OFF[i]:_OFF[i + 1]] for i, n in enumerate(names)}


def _column_scales():
    scale = np.ones((_OFF[-1],), np.float32)
    names = {"a_q": 0, "b_q": 4, "c_q": 11}
    for i in names.values():
        scale[_OFF[i]:_OFF[i + 1]] = HEAD_DIM ** -0.5 * LOG2_E
    scale[_OFF[10]:_OFF[11]] = (B_IDX_HEADS * B_IDX_DIM) ** -0.5
    return scale


def _layer_weights(w_bf):
    p = _split_w_in(w_bf)
    main = jnp.concatenate([
        p["a_q"], p["a_k"], p["b_q"], p["b_k"], p["b_k"],
        p["i_q"], p["i_k"], p["i_k"], p["c_q"], p["c_k"], p["c_v"]], axis=1)
    w_t = jnp.concatenate([
        p["a_v"].T, p["b_v"].T, p["b_v"].T, p["i_w"].T,
        jnp.zeros((IWT_ROWS - B_IDX_HEADS, D_MODEL), w_bf.dtype)], axis=0)
    gates = jnp.concatenate([p["a_g"], p["b_g"], p["c_g"], p["m_g"]], axis=1)
    return main, w_t, gates


def kernel(x, norm_g, w_in, w_br_a, w_br_b, w_br_c, w_out, final_norm_g):
    bsz, seq, d_model = x.shape
    assert seq == SEQ and d_model == D_MODEL
    assert MAIN_WIDTH % LANES == 0 and GATE_WIDTH == sum(IN_SPLITS[i] for i in (3, 7, 14, 15))
    cos_t, sin_t = _rope_tables()
    bf = jnp.bfloat16
    x2 = x.reshape(bsz * seq, d_model)
    fg = final_norm_g.reshape(1, d_model)
    w_in_bf = (w_in * _column_scales()).astype(bf)
    for layer in range(DEPTH):
        w_main, w_t, w_gate = _layer_weights(w_in_bf[layer])
        g = norm_g[layer].reshape(1, d_model)
        main, vt, iwt = _proj(x2, g, w_main, w_t, cos_t, sin_t)
        ya = _moba(main, vt, bsz)
        yb = _dsa(main, vt, iwt, bsz)
        yc = _dilated(main, bsz)
        x2 = _merge(x2, g, ya, yb, yc, w_gate, w_br_a[layer].astype(bf), w_br_b[layer].astype(bf),
                    w_br_c[layer].astype(bf), w_out[layer].astype(bf), fg, final=(layer == DEPTH - 1))
    return x2.reshape(bsz, seq, d_model)
```

```python
import functools

import numpy as np
import jax
import jax.numpy as jnp
from jax import lax
from jax.experimental import pallas as pl
from jax.experimental.pallas import tpu as pltpu

D_MODEL = 1024
SEQ = 2048
DEPTH = 2
HEAD_DIM = 64
ROPE_THETA = 10000.0
RMS_EPS = 1e-6
N_BRANCH = 3

A_HEADS = 6
A_WIDTH = A_HEADS * HEAD_DIM
A_BLOCK = 256
A_TOPK = 3

B_HEADS = 6
B_WIDTH = B_HEADS * HEAD_DIM
B_TOPK = 256
B_IDX_HEADS = 4
B_IDX_DIM = HEAD_DIM

C_GROUPS = ((128, 1), (512, 4), (2048, 16))
C_SLOTS = 4
C_HEADS = C_SLOTS * len(C_GROUPS)
C_WIDTH = C_SLOTS * HEAD_DIM
C_BLOCK = 128

IN_SPLITS = (A_WIDTH, A_WIDTH, A_WIDTH, A_WIDTH,
             B_WIDTH, HEAD_DIM, HEAD_DIM, B_WIDTH, B_IDX_HEADS * B_IDX_DIM, B_IDX_DIM, B_IDX_HEADS,
             C_HEADS * HEAD_DIM, C_HEADS * HEAD_DIM, C_HEADS * HEAD_DIM, C_WIDTH,
             N_BRANCH * D_MODEL)
_OFF = tuple(int(o) for o in np.cumsum((0,) + IN_SPLITS))

LANES = 128
VMEM_LIMIT = 56 * 1024 * 1024

_MAIN_GROUPS = (("a_q", A_WIDTH), ("a_k", A_WIDTH), ("b_q", B_WIDTH), ("b_kk", LANES),
                ("i_q", B_IDX_HEADS * B_IDX_DIM), ("i_kk", LANES),
                ("c_q", C_HEADS * HEAD_DIM), ("c_k", C_HEADS * HEAD_DIM), ("c_v", C_HEADS * HEAD_DIM))
_MAIN_OFF = {}
_c = 0
for _n, _w in _MAIN_GROUPS:
    _MAIN_OFF[_n] = _c
    _c += _w
MAIN_WIDTH = _c
ROPE_WIDTH = _MAIN_OFF["c_v"]
VT_ROWS = A_WIDTH + LANES
IWT_ROWS = 8
GATE_WIDTH = A_WIDTH + B_WIDTH + C_WIDTH + N_BRANCH * D_MODEL

TM_PROJ = 1024
TM_MERGE = 1024
TQ_DSA = 128
INT_MIN = -2 ** 31
LOG2_E = 1.4426950408889634
NEG_INF = float("-inf")


def _dot(a, b):
    return jnp.dot(a, b, preferred_element_type=jnp.float32)


def _dot_nt(a, b):
    return lax.dot_general(a, b, (((1,), (1,)), ((), ())), preferred_element_type=jnp.float32)


def _colsum(x):
    n = x.shape[0]
    part = x.reshape(n // LANES, LANES, x.shape[1]).sum(axis=0)
    return part.sum(axis=0, keepdims=True)


def _colmax(x):
    n = x.shape[0]
    part = x.reshape(n // LANES, LANES, x.shape[1]).max(axis=0)
    return part.max(axis=0, keepdims=True)


def _rms(x, g):
    return x * lax.rsqrt(jnp.mean(x * x, axis=-1, keepdims=True) + RMS_EPS) * g


def _head_mask(hh):
    lane = lax.broadcasted_iota(jnp.int32, (1, LANES), 1)
    return (lane >= hh * HEAD_DIM) & (lane < (hh + 1) * HEAD_DIM)


def _proj_kernel(x_ref, g_ref, w_ref, wt_ref, cos_ref, sin_ref, main_ref, vt_ref, iwt_ref):
    h = _rms(x_ref[...], g_ref[...]).astype(jnp.bfloat16)
    cos = cos_ref[...]
    sin = sin_ref[...]
    lane = lax.broadcasted_iota(jnp.int32, (1, LANES), 1)
    first_half = (lane % HEAD_DIM) < (HEAD_DIM // 2)
    chunk = 4 * LANES
    for c0 in range(0, MAIN_WIDTH, chunk):
        w = min(chunk, MAIN_WIDTH - c0)
        res = _dot(h, w_ref[:, c0:c0 + w])
        for j in range(0, w, LANES):
            blk = res[:, j:j + LANES]
            if c0 + j < ROPE_WIDTH:
                partner = jnp.where(first_half, pltpu.roll(blk, LANES - HEAD_DIM // 2, 1),
                                    pltpu.roll(blk, HEAD_DIM // 2, 1))
                blk = blk * cos + partner * sin
            main_ref[:, c0 + j:c0 + j + LANES] = blk.astype(main_ref.dtype)
    rt = _dot_nt(wt_ref[...], h)
    vt_ref[...] = rt[:VT_ROWS].astype(vt_ref.dtype)
    iwt_ref[...] = rt[VT_ROWS:]


def _proj(x2, g, w_main, w_t, cos_t, sin_t):
    t = x2.shape[0]
    n_rope_tiles = SEQ // TM_PROJ
    return pl.pallas_call(
        _proj_kernel,
        grid=(t // TM_PROJ,),
        in_specs=[
            pl.BlockSpec((TM_PROJ, D_MODEL), lambda i: (i, 0)),
            pl.BlockSpec((1, D_MODEL), lambda i: (0, 0), pipeline_mode=pl.Buffered(1)),
            pl.BlockSpec((D_MODEL, MAIN_WIDTH), lambda i: (0, 0), pipeline_mode=pl.Buffered(1)),
            pl.BlockSpec((VT_ROWS + IWT_ROWS, D_MODEL), lambda i: (0, 0), pipeline_mode=pl.Buffered(1)),
            pl.BlockSpec((TM_PROJ, LANES), lambda i: (i % n_rope_tiles, 0)),
            pl.BlockSpec((TM_PROJ, LANES), lambda i: (i % n_rope_tiles, 0)),
        ],
        out_specs=[
            pl.BlockSpec((TM_PROJ, MAIN_WIDTH), lambda i: (i, 0)),
            pl.BlockSpec((VT_ROWS, TM_PROJ), lambda i: (0, i)),
            pl.BlockSpec((IWT_ROWS, TM_PROJ), lambda i: (0, i)),
        ],
        out_shape=[
            jax.ShapeDtypeStruct((t, MAIN_WIDTH), jnp.bfloat16),
            jax.ShapeDtypeStruct((VT_ROWS, t), jnp.bfloat16),
            jax.ShapeDtypeStruct((IWT_ROWS, t), jnp.float32),
        ],
        compiler_params=pltpu.CompilerParams(dimension_semantics=("parallel",),
                                             vmem_limit_bytes=VMEM_LIMIT),
        name="proj",
    )(x2, g, w_main, w_t, cos_t, sin_t)


N_ABLK = SEQ // A_BLOCK


def _moba_kernel(q_ref, k_ref, vt_ref, o_ref):
    kmean = jnp.concatenate(
        [jnp.mean(k_ref[n * A_BLOCK:(n + 1) * A_BLOCK, :].astype(jnp.float32), axis=0, keepdims=True)
         for n in range(N_ABLK)], axis=0).astype(jnp.bfloat16)
    for own in range(N_ABLK):
        _moba_body(own, kmean, q_ref, k_ref, vt_ref, o_ref)


def _moba_body(own, kmean, q_ref, k_ref, vt_ref, o_ref):
    nk = (own + 1) * A_BLOCK
    q = q_ref[own * A_BLOCK:nk, :]
    cols = 2 * A_BLOCK
    blk_f = lax.broadcasted_iota(jnp.int32, (N_ABLK, cols), 0).astype(jnp.float32)
    kpos = lax.broadcasted_iota(jnp.int32, (A_BLOCK, cols), 0)
    qpos = lax.broadcasted_iota(jnp.int32, (A_BLOCK, cols), 1) & (A_BLOCK - 1)
    causal = kpos <= qpos
    q2 = jnp.concatenate([jnp.where(_head_mask(hh), q, jnp.zeros_like(q)) for hh in range(2)], axis=0)
    gate = _dot_nt(kmean, q2)
    gate = jnp.where(blk_f < float(own), gate, NEG_INF)
    sel = jnp.zeros(gate.shape, jnp.float32)
    for _ in range(min(A_TOPK, own)):
        top = jnp.max(gate, axis=0, keepdims=True)
        is_top = (gate == top) & (top > NEG_INF)
        idx = jnp.min(jnp.where(is_top, blk_f, float(N_ABLK)), axis=0, keepdims=True)
        pick = blk_f == idx
        sel = jnp.where(pick, 1.0, sel)
        gate = jnp.where(pick, NEG_INF, gate)
    s = _dot_nt(k_ref[0:nk, :], q2)
    parts = [jnp.where(sel[n:n + 1, :] > 0.0, s[n * A_BLOCK:(n + 1) * A_BLOCK], NEG_INF)
             for n in range(own)]
    parts.append(jnp.where(causal, s[own * A_BLOCK:], NEG_INF))
    s = jnp.concatenate(parts, axis=0)
    m = _colmax(s)
    p = jnp.exp2(s - m)
    l = _colsum(p)
    o = _dot(vt_ref[:, 0:nk], p.astype(jnp.bfloat16)) / l
    o = jnp.concatenate([o[hh * HEAD_DIM:(hh + 1) * HEAD_DIM, hh * A_BLOCK:(hh + 1) * A_BLOCK]
                         for hh in range(2)], axis=0)
    o_ref[own * A_BLOCK:nk, :] = o.T.astype(o_ref.dtype)


def _moba(main, vt, bsz):
    t = main.shape[0]
    n_pairs = A_HEADS // 2
    q_col = _MAIN_OFF["a_q"] // LANES
    k_col = _MAIN_OFF["a_k"] // LANES
    return pl.pallas_call(
        _moba_kernel,
        grid=(bsz, n_pairs),
        in_specs=[
            pl.BlockSpec((SEQ, LANES), lambda b, p: (b, q_col + p)),
            pl.BlockSpec((SEQ, LANES), lambda b, p: (b, k_col + p)),
            pl.BlockSpec((LANES, SEQ), lambda b, p: (p, b)),
        ],
        out_specs=pl.BlockSpec((SEQ, LANES), lambda b, p: (b, p)),
        out_shape=jax.ShapeDtypeStruct((t, A_WIDTH), jnp.bfloat16),
        compiler_params=pltpu.CompilerParams(
            dimension_semantics=("parallel", "parallel"), vmem_limit_bytes=VMEM_LIMIT),
        name="moba",
    )(main, main, vt)


TIE_CHUNK = 256


KV_STEP = 256


def _dsa_kernel(ikk_ref, kk_ref, vvt_ref, iq_ref, iwt_ref, q_ref, o_ref, *scratch):
    half = pl.program_id(1)
    n_qt = SEQ // TQ_DSA
    tiles_per_step = KV_STEP // TQ_DSA
    n_var = SEQ // KV_STEP
    for c in range(n_var // 2):
        j = tiles_per_step * c + half
        u_a, bias_a, u_b, bias_b = scratch[4 * c:4 * c + 4]
        tiles = (((c + 1) * KV_STEP, j, u_a, bias_a), ((n_var - c) * KV_STEP, n_qt - 1 - j, u_b, bias_b))
        _dsa_step(tiles, ikk_ref, kk_ref, vvt_ref, iq_ref, iwt_ref, q_ref, o_ref)


def _tile_rows(qt):
    return pl.ds(pl.multiple_of(qt * TQ_DSA, TQ_DSA), TQ_DSA)


def _dsa_step(tiles, ikk_ref, kk_ref, vvt_ref, iq_ref, iwt_ref, q_ref, o_ref):
    searched = []
    for nk, qt, u_ref, bias_ref in tiles:
        key_pos = lax.broadcasted_iota(jnp.int32, (nk, TQ_DSA), 0)
        q_pos = qt * TQ_DSA + lax.broadcasted_iota(jnp.int32, (nk, TQ_DSA), 1)
        causal = key_pos <= q_pos
        if nk <= B_TOPK:
            bias_ref[0:nk, :] = jnp.where(causal, 0.0, NEG_INF)
        else:
            _dsa_scores(nk, causal, ikk_ref, iq_ref[_tile_rows(qt), :], iwt_ref[qt], u_ref)
            searched.append((nk, u_ref, bias_ref))
    found = _dsa_thresholds([(nk, u_ref) for nk, u_ref, _ in searched])
    for (nk, u_ref, bias_ref), (tau, n_ge) in zip(searched, found):
        _dsa_mask(nk, tau, n_ge, u_ref, bias_ref)
    for nk, qt, _, bias_ref in tiles:
        _dsa_attend(nk, qt, kk_ref, vvt_ref, q_ref[_tile_rows(qt), :], bias_ref, o_ref)


def _dsa_attend(nk, qt, kk_ref, vvt_ref, q, bias_ref, o_ref):
    q_all = _stack_heads(q, B_HEADS)
    bias = bias_ref[0:nk, :]
    s = _dot_nt(kk_ref[0:nk, :], q_all) + jnp.concatenate([bias] * B_HEADS, axis=1)
    m = _colmax(s)
    p = jnp.exp2(s - m)
    l = _colsum(p)
    o = _dot(vvt_ref[:, 0:nk], p.astype(jnp.bfloat16)) / l
    o = jnp.concatenate([o[(h % 2) * HEAD_DIM:(h % 2 + 1) * HEAD_DIM, h * TQ_DSA:(h + 1) * TQ_DSA]
                         for h in range(B_HEADS)], axis=0)
    o_ref[_tile_rows(qt), :] = o.T.astype(o_ref.dtype)


def _dsa_scores(nk, causal, ikk_ref, iq, iw, u_ref):
    iq_all = _stack_heads(iq, B_IDX_HEADS)
    iw_all = jnp.concatenate([iw[h:h + 1, :] for h in range(B_IDX_HEADS)], axis=1)
    part = jnp.maximum(_dot_nt(ikk_ref[0:nk, :], iq_all), 0.0) * iw_all
    score = part[:, 0:TQ_DSA]
    for h in range(1, B_IDX_HEADS):
        score = score + part[:, h * TQ_DSA:(h + 1) * TQ_DSA]

    score = jnp.where(score == 0.0, 0.0, score)
    bits = pltpu.bitcast(score, jnp.int32)
    u = jnp.where(bits < 0, bits ^ jnp.int32(0x7FFFFFFF), bits)
    u_ref[0:nk, :] = jnp.where(causal, u, jnp.int32(INT_MIN))


def _dsa_thresholds(jobs):
    n_keep = float(B_TOPK)

    def bisect(i, carry):
        bit = jnp.left_shift(jnp.int32(1), 31 - i)
        out = []
        for (nk, u_ref), (tau, n_ge) in zip(jobs, carry):
            trial = tau ^ bit
            acc = jnp.zeros((LANES, TQ_DSA), jnp.float32)
            for c in range(nk // LANES):
                acc = jnp.where(u_ref[c * LANES:(c + 1) * LANES, :] >= trial, acc + 1.0, acc)
            cnt = acc.sum(axis=0, keepdims=True)
            ok = cnt >= n_keep
            out.append((jnp.where(ok, trial, tau), jnp.where(ok, cnt, n_ge)))
        return tuple(out)

    init = tuple((jnp.full((1, TQ_DSA), INT_MIN, jnp.int32), jnp.full((1, TQ_DSA), float(nk), jnp.float32))
                 for nk, _ in jobs)
    return lax.fori_loop(0, 32, bisect, init)


def _dsa_mask(nk, tau, n_ge, u_ref, bias_ref):
    far = 1e9
    excess = jnp.where(tau > jnp.int32(INT_MIN), n_ge - float(B_TOPK), far)
    r_i = lax.broadcasted_iota(jnp.int32, (TIE_CHUNK, TIE_CHUNK), 0)
    c_i = lax.broadcasted_iota(jnp.int32, (TIE_CHUNK, TIE_CHUNK), 1)
    above = jnp.where(c_i > r_i, 1.0, 0.0).astype(jnp.bfloat16)
    later = jnp.zeros((1, TQ_DSA), jnp.float32)
    for c in reversed(range(nk // TIE_CHUNK)):
        uc = u_ref[c * TIE_CHUNK:(c + 1) * TIE_CHUNK, :]
        eq = uc == tau
        eq_f = jnp.where(eq, 1.0, 0.0)
        after = _dot(above, eq_f.astype(jnp.bfloat16)) + later
        later = later + _colsum(eq_f)
        slack = jnp.where(uc > tau, far, jnp.where(eq, after, -1.0))
        bias_ref[c * TIE_CHUNK:(c + 1) * TIE_CHUNK, :] = jnp.where(slack >= excess, 0.0, NEG_INF)


def _stack_heads(x, n_heads):
    rows = []
    for h in range(n_heads):
        blk = x[:, (h // 2) * LANES:(h // 2 + 1) * LANES]
        rows.append(jnp.where(_head_mask(h % 2), blk, jnp.zeros_like(blk)))
    return jnp.concatenate(rows, axis=0)


def _dsa(main, vt, iwt, bsz):
    t = main.shape[0]
    n_qt = SEQ // TQ_DSA
    n_var = SEQ // KV_STEP
    iwt_tiles = iwt.reshape(IWT_ROWS, t // TQ_DSA, TQ_DSA).transpose(1, 0, 2)
    scratch = []
    for c in range(n_var // 2):
        for nk in ((c + 1) * KV_STEP, (n_var - c) * KV_STEP):
            scratch += [pltpu.VMEM((nk, TQ_DSA), jnp.int32), pltpu.VMEM((nk, TQ_DSA), jnp.float32)]
    return pl.pallas_call(
        _dsa_kernel,
        grid=(bsz, KV_STEP // TQ_DSA),
        in_specs=[
            pl.BlockSpec((SEQ, LANES), lambda b, i: (b, _MAIN_OFF["i_kk"] // LANES)),
            pl.BlockSpec((SEQ, LANES), lambda b, i: (b, _MAIN_OFF["b_kk"] // LANES)),
            pl.BlockSpec((LANES, SEQ), lambda b, i: (A_WIDTH // LANES, b)),
            pl.BlockSpec((SEQ, 2 * LANES), lambda b, i: (b, _MAIN_OFF["i_q"] // (2 * LANES))),
            pl.BlockSpec((n_qt, IWT_ROWS, TQ_DSA), lambda b, i: (b, 0, 0)),
            pl.BlockSpec((SEQ, B_WIDTH), lambda b, i: (b, _MAIN_OFF["b_q"] // B_WIDTH)),
        ],
        out_specs=pl.BlockSpec((SEQ, B_WIDTH), lambda b, i: (b, 0)),
        out_shape=jax.ShapeDtypeStruct((t, B_WIDTH), jnp.bfloat16),
        scratch_shapes=scratch,
        compiler_params=pltpu.CompilerParams(
            dimension_semantics=("parallel", "arbitrary"), vmem_limit_bytes=VMEM_LIMIT),
        name="dsa",
    )(main, main, vt, main, iwt_tiles, main)


N_CBLK = SEQ // C_BLOCK
N_GROUPS = len(C_GROUPS)
C_SLAB = 4


def _dilated_kernel(*refs):
    q_in = refs[0:N_GROUPS]
    k_in = refs[N_GROUPS:2 * N_GROUPS]
    v_in = refs[2 * N_GROUPS:3 * N_GROUPS]
    o_ref = refs[3 * N_GROUPS]
    scratch = refs[3 * N_GROUPS + 1:]
    tmp_ref = scratch[0]
    dense = scratch[1:1 + 3 * (N_GROUPS - 1)]
    od_ref, ld_ref = scratch[-2 - 2 * N_GROUPS], scratch[-1 - 2 * N_GROUPS]
    o_tok = scratch[-2 * N_GROUPS:-N_GROUPS]
    l_tok = scratch[-N_GROUPS:]

    lane = lax.broadcasted_iota(jnp.int32, (1, LANES), 1)
    low = lane < HEAD_DIM
    qi = lax.broadcasted_iota(jnp.int32, (C_BLOCK, C_BLOCK), 0)
    ki = lax.broadcasted_iota(jnp.int32, (C_BLOCK, C_BLOCK), 1)
    own_ok = ki <= qi

    for g, (window, dil) in enumerate(C_GROUPS):
        assert window // dil == C_BLOCK
        n_sub = SEQ // dil
        n_blk = n_sub // C_BLOCK
        if dil == 1:
            qd, kd, vd = q_in[g], k_in[g], v_in[g]
        else:
            qd, kd, vd = dense[3 * (g - 1):3 * g]
            for src, dst in ((q_in[g], qd), (k_in[g], kd), (v_in[g], vd)):
                tmp_ref[...] = src[...].astype(jnp.float32)
                for r in range(dil):
                    dst[r * n_sub:(r + 1) * n_sub, :] = tmp_ref[pl.ds(r, n_sub, stride=dil), :].astype(dst.dtype)
        o_dst = o_tok[g] if dil == 1 else od_ref
        l_dst = l_tok[g] if dil == 1 else ld_ref

        blocked = (N_CBLK, C_BLOCK, LANES)
        q3 = qd[...].reshape(blocked)
        k3 = kd[...].reshape(blocked)
        v3 = vd[...].reshape(blocked)
        if n_blk > 1:
            pad = jnp.zeros((1, C_BLOCK, LANES), k3.dtype)
            k3 = jnp.concatenate([jnp.concatenate([pad, k3[:-1]], axis=0), k3], axis=1)
            v3 = jnp.concatenate([jnp.concatenate([pad, v3[:-1]], axis=0), v3], axis=1)
            shape = (N_CBLK, C_BLOCK, 2 * C_BLOCK)
            b_i = lax.broadcasted_iota(jnp.int32, shape, 0)
            q_i = lax.broadcasted_iota(jnp.int32, shape, 1)
            k_i = lax.broadcasted_iota(jnp.int32, shape, 2)
            lo = jnp.minimum(q_i + jnp.where((b_i & (n_blk - 1)) == 0, C_BLOCK, 0), C_BLOCK)
            bias = jnp.where(k_i >= lo, jnp.where(k_i <= q_i + C_BLOCK, 0.0, NEG_INF), NEG_INF)
        else:
            bias = jnp.where(own_ok, 0.0, NEG_INF)[None]
        for b0 in range(0, N_CBLK, C_SLAB):
            slab = slice(b0, b0 + C_SLAB)
            rows = slice(b0 * C_BLOCK, (b0 + C_SLAB) * C_BLOCK)
            q_s, k_s, v_s = q3[slab], k3[slab], v3[slab]
            bias_s = bias if bias.shape[0] == 1 else bias[slab]
            o_h, l_h = [], []
            for hh in range(2):
                qh = jnp.where(_head_mask(hh)[None], q_s, jnp.zeros_like(q_s))
                s = jnp.einsum("bqd,bkd->bqk", qh, k_s, preferred_element_type=jnp.float32) + bias_s
                m = jnp.max(s, axis=-1, keepdims=True)
                p = jnp.exp2(s - m)
                l = jnp.sum(p, axis=-1, keepdims=True)
                o = jnp.einsum("bqk,bkd->bqd", p.astype(jnp.bfloat16), v_s, preferred_element_type=jnp.float32)
                o_h.append(o / l)
                l_h.append(m + jnp.log2(l))
            o_dst[rows, :] = jnp.where(low[None], o_h[0], o_h[1]).reshape(C_SLAB * C_BLOCK, LANES)
            l_dst[rows, :] = jnp.where(low[None], l_h[0], l_h[1]).reshape(C_SLAB * C_BLOCK, LANES)
        if dil > 1:
            for r in range(dil):
                o_tok[g][pl.ds(r, n_sub, stride=dil), :] = od_ref[r * n_sub:(r + 1) * n_sub, :]
                l_tok[g][pl.ds(r, n_sub, stride=dil), :] = ld_ref[r * n_sub:(r + 1) * n_sub, :]

    lses = [l_tok[g][...] for g in range(N_GROUPS)]
    top = functools.reduce(jnp.maximum, lses)
    es = [jnp.exp2(x - top) for x in lses]
    num = sum(e * o_tok[g][...] for g, e in enumerate(es))
    o_ref[...] = (num / sum(es)).astype(o_ref.dtype)


def _dilated(main, bsz):
    t = main.shape[0]
    n_pairs = C_SLOTS // 2

    def col_spec(name, g):
        base = _MAIN_OFF[name] // LANES + g * n_pairs
        return pl.BlockSpec((SEQ, LANES), lambda b, p: (b, base + p))

    in_specs = ([col_spec("c_q", g) for g in range(N_GROUPS)]
                + [col_spec("c_k", g) for g in range(N_GROUPS)]
                + [col_spec("c_v", g) for g in range(N_GROUPS)])
    scratch = ([pltpu.VMEM((SEQ, LANES), jnp.float32)]
               + [pltpu.VMEM((SEQ, LANES), jnp.bfloat16)] * (3 * (N_GROUPS - 1))
               + [pltpu.VMEM((SEQ, LANES), jnp.float32)] * (2 + 2 * N_GROUPS))
    return pl.pallas_call(
        _dilated_kernel,
        grid=(bsz, n_pairs),
        in_specs=in_specs,
        out_specs=pl.BlockSpec((SEQ, LANES), lambda b, p: (b, p)),
        out_shape=jax.ShapeDtypeStruct((t, C_WIDTH), jnp.bfloat16),
        scratch_shapes=scratch,
        compiler_params=pltpu.CompilerParams(
            dimension_semantics=("parallel", "parallel"), vmem_limit_bytes=VMEM_LIMIT),
        name="dilated",
    )(*([main] * (3 * N_GROUPS)))


def _sigmoid(x):
    return 1.0 / (1.0 + jnp.exp(-x))


def _merge_kernel(x_ref, g_ref, ya_ref, yb_ref, yc_ref, wg_ref, wa_ref, wb_ref, wc_ref, wo_ref, fg_ref,
                  o_ref, *, final):
    x = x_ref[...]
    h = _rms(x, g_ref[...]).astype(jnp.bfloat16)
    merged = jnp.zeros(x.shape, jnp.float32)
    c0 = 0
    m0 = A_WIDTH + B_WIDTH + C_WIDTH
    for i, (y_ref, w_ref) in enumerate(((ya_ref, wa_ref), (yb_ref, wb_ref), (yc_ref, wc_ref))):
        width = y_ref.shape[1]
        gate = _dot(h, wg_ref[:, c0:c0 + width])
        z = (y_ref[...].astype(jnp.float32) * (gate * _sigmoid(gate))).astype(jnp.bfloat16)
        mix = _sigmoid(_dot(h, wg_ref[:, m0 + i * D_MODEL:m0 + (i + 1) * D_MODEL]))
        merged = merged + mix * _dot(z, w_ref[...])
        c0 += width
    out = x + _dot(merged.astype(jnp.bfloat16), wo_ref[...])
    if final:
        out = _rms(out, fg_ref[...])
    o_ref[...] = out


def _merge(x2, g, ya, yb, yc, w_gate, w_a, w_b, w_c, w_o, fg, final):
    t = x2.shape[0]
    row = lambda w: pl.BlockSpec((TM_MERGE, w), lambda i: (i, 0))
    full = lambda a: pl.BlockSpec(a.shape, lambda i: (0, 0), pipeline_mode=pl.Buffered(1))
    return pl.pallas_call(
        functools.partial(_merge_kernel, final=final),
        grid=(t // TM_MERGE,),
        in_specs=[row(D_MODEL), full(g), row(A_WIDTH), row(B_WIDTH), row(C_WIDTH),
                  full(w_gate), full(w_a), full(w_b), full(w_c), full(w_o), full(fg)],
        out_specs=row(D_MODEL),
        out_shape=jax.ShapeDtypeStruct((t, D_MODEL), jnp.float32),
        compiler_params=pltpu.CompilerParams(dimension_semantics=("parallel",),
                                             vmem_limit_bytes=VMEM_LIMIT),
        name="merge_final" if final else "merge",
    )(x2, g, ya, yb, yc, w_gate, w_a, w_b, w_c, w_o, fg)


def _rope_tables():
    inv_freq = 1.0 / (ROPE_THETA ** (jnp.arange(0, HEAD_DIM, 2, dtype=jnp.float32) / HEAD_DIM))
    ang = jnp.arange(SEQ, dtype=jnp.float32)[:, None] * inv_freq[None, :]
    cos, sin = jnp.cos(ang), jnp.sin(ang)
    reps = LANES // HEAD_DIM
    cos_t = jnp.tile(jnp.concatenate([cos, cos], axis=1), (1, reps))
    sin_t = jnp.tile(jnp.concatenate([-sin, sin], axis=1), (1, reps))
    return cos_t, sin_t


def _split_w_in(w):
    names = ("a_q", "a_k", "a_v", "a_g", "b_q", "b_k", "b_v", "b_g", "i_q", "i_k", "i_w",
             "c_q", "c_k", "c_v", "c_g", "m_g")
    return {n: w[:, _OFF[i]:_OFF[i + 1]] for i, n in enumerate(names)}


def _column_scales():
    scale = np.ones((_OFF[-1],), np.float32)
    names = {"a_q": 0, "b_q": 4, "c_q": 11}
    for i in names.values():
        scale[_OFF[i]:_OFF[i + 1]] = HEAD_DIM ** -0.5 * LOG2_E
    scale[_OFF[10]:_OFF[11]] = (B_IDX_HEADS * B_IDX_DIM) ** -0.5
    return scale


def _layer_weights(w_bf):
    p = _split_w_in(w_bf)
    main = jnp.concatenate([
        p["a_q"], p["a_k"], p["b_q"], p["b_k"], p["b_k"],
        p["i_q"], p["i_k"], p["i_k"], p["c_q"], p["c_k"], p["c_v"]], axis=1)
    w_t = jnp.concatenate([
        p["a_v"].T, p["b_v"].T, p["b_v"].T, p["i_w"].T,
        jnp.zeros((IWT_ROWS - B_IDX_HEADS, D_MODEL), w_bf.dtype)], axis=0)
    gates = jnp.concatenate([p["a_g"], p["b_g"], p["c_g"], p["m_g"]], axis=1)
    return main, w_t, gates


def kernel(x, norm_g, w_in, w_br_a, w_br_b, w_br_c, w_out, final_norm_g):
    bsz, seq, d_model = x.shape
    assert seq == SEQ and d_model == D_MODEL
    assert MAIN_WIDTH % LANES == 0 and GATE_WIDTH == sum(IN_SPLITS[i] for i in (3, 7, 14, 15))
    cos_t, sin_t = _rope_tables()
    bf = jnp.bfloat16
    x2 = x.reshape(bsz * seq, d_model)
    fg = final_norm_g.reshape(1, d_model)
    w_in_bf = (w_in * _column_scales()).astype(bf)
    for layer in range(DEPTH):
        w_main, w_t, w_gate = _layer_weights(w_in_bf[layer])
        g = norm_g[layer].reshape(1, d_model)
        main, vt, iwt = _proj(x2, g, w_main, w_t, cos_t, sin_t)
        ya = _moba(main, vt, bsz)
        yb = _dsa(main, vt, iwt, bsz)
        yc = _dilated(main, bsz)
        x2 = _merge(x2, g, ya, yb, yc, w_gate, w_br_a[layer].astype(bf), w_br_b[layer].astype(bf),
                    w_br_c[layer].astype(bf), w_out[layer].astype(bf), fg, final=(layer == DEPTH - 1))
    return x2.reshape(bsz, seq, d_model)
```
